```python
import jax, jax.numpy as jnp
from jax import lax
import numpy as np

D_MODEL = 1024
BATCH = 8
SEQ = 8192
DEPTH = 2
DEC_BATCH = 2
DEC_SEQ = 16384
PAST_LEN = 128

N_EVEN = (DEPTH + 1) // 2
N_ODD = DEPTH // 2
D_A = 512
D_B = 512
K_A = 3
K_B = 31
D_C = 768
H_C = 6
DH_C = D_C // H_C
CHUNK = 128
D_D = 256
G_D = 4
DG_D = D_D // G_D
D_FF = ((8 * D_MODEL // 3 + 255) // 256) * 256
N_MOD = 6
EPS = 1e-6

kernel_name = "hybrid_bidir_conv_gmlp_fnet_encoder"


def rmsnorm(x, g):
    x32 = x.astype(jnp.float32)
    y = x32 * lax.rsqrt(jnp.mean(x32 * x32, axis=-1, keepdims=True) + EPS)
    return y.astype(x.dtype) * g


def layernorm(x, g, b):
    x32 = x.astype(jnp.float32)
    mu = jnp.mean(x32, axis=-1, keepdims=True)
    var = jnp.mean(jnp.square(x32 - mu), axis=-1, keepdims=True)
    y = (x32 - mu) * lax.rsqrt(var + EPS)
    return y.astype(x.dtype) * g + b


def depthwise_conv(x, w):
    k, ch = w.shape
    return lax.conv_general_dilated(
        x, w[:, None, :].astype(x.dtype), window_strides=(1,), padding=[(k // 2, k // 2)],
        dimension_numbers=("NWC", "WIO", "NWC"), feature_group_count=ch)


def mixer_ab(h, w_in, conv_a, conv_b_w, conv_b_b, ln_g, ln_b, w_out):
    p = h @ w_in
    a_b, a_c, a_x, b_val, b_gate = jnp.split(
        p, [D_A, 2 * D_A, 3 * D_A, 3 * D_A + D_B], axis=-1)
    y_a = a_b * depthwise_conv(a_c * a_x, conv_a)
    g = b_val * jax.nn.sigmoid(b_gate)
    z = depthwise_conv(g, conv_b_w) + conv_b_b
    y_b = jax.nn.silu(layernorm(z, ln_g, ln_b))
    return jnp.concatenate([y_a, y_b], axis=-1) @ w_out


def mixer_cd(h, w_in, ln_g, ln_b, w_s, b_s, w_out):
    bsz, s, _ = h.shape
    p = h @ w_in
    u, v, f = jnp.split(p, [D_C, 2 * D_C], axis=-1)
    v = layernorm(v, ln_g, ln_b)
    vc = v.reshape(bsz, s // CHUNK, CHUNK, H_C, DH_C)
    sv = jnp.einsum('hpq,bnqhd->bnphd', w_s, vc) + b_s.T[None, None, :, :, None]
    y_c = u * sv.reshape(bsz, s, D_C)
    fg = f.reshape(bsz, s, G_D, DG_D).astype(jnp.float32)
    y_d = jnp.fft.fftn(fg, axes=(1, 3), norm="ortho").real.astype(h.dtype).reshape(bsz, s, D_D)
    return jnp.concatenate([y_c, y_d], axis=-1) @ w_out


def swiglu(h, w_in, w_out):
    gate, up = jnp.split(h @ w_in, 2, axis=-1)
    return (jax.nn.silu(gate) * up) @ w_out


def trunk(x, c, ada_w, ada_b, mix_norm_g, ffn_norm_g,
          ab_w_in, ab_conv_a, ab_conv_b_w, ab_conv_b_b, ab_ln_g, ab_ln_b, ab_w_out,
          cd_w_in, cd_ln_g, cd_ln_b, cd_w_s, cd_b_s, cd_w_out,
          ffn_w_in, ffn_w_out, final_g):
    sc = jax.nn.silu(c)
    for l in range(DEPTH):
        mod = (sc @ ada_w[l] + ada_b[l])[:, None, :]
        sh1, sc1, g1, sh2, sc2, g2 = jnp.split(mod, N_MOD, axis=-1)
        h = rmsnorm(x, mix_norm_g[l]) * (1 + sc1) + sh1
        if l % 2 == 0:
            i = l // 2
            m = mixer_ab(h, ab_w_in[i], ab_conv_a[i], ab_conv_b_w[i], ab_conv_b_b[i],
                         ab_ln_g[i], ab_ln_b[i], ab_w_out[i])
        else:
            i = l // 2
            m = mixer_cd(h, cd_w_in[i], cd_ln_g[i], cd_ln_b[i], cd_w_s[i], cd_b_s[i], cd_w_out[i])
        x = x + g1 * m
        h = rmsnorm(x, ffn_norm_g[l]) * (1 + sc2) + sh2
        x = x + g2 * swiglu(h, ffn_w_in[l], ffn_w_out[l])
    return rmsnorm(x, final_g)


def setup_inputs(seed: int = 0) -> dict:
    key = jax.random.key(seed)
    ks = jax.random.split(key, 32)
    f32 = jnp.float32
    D = D_MODEL

    def nrm(k, shape, scale):
        return jax.random.normal(k, shape, f32) * scale

    return {
        "x_prompt": nrm(ks[0], (BATCH, SEQ, D), 1.0),
        "x_sample": nrm(ks[1], (DEC_BATCH, DEC_SEQ, D), 1.0),
        "c_prompt": nrm(ks[2], (BATCH, D), 1.0),
        "c_sample": nrm(ks[3], (DEC_BATCH, D), 1.0),
        "ada_w": nrm(ks[4], (DEPTH, D, N_MOD * D), D ** -0.5),
        "ada_b": nrm(ks[5], (DEPTH, N_MOD * D), 0.02),
        "mix_norm_g": 1.0 + nrm(ks[6], (DEPTH, D), 0.02),
        "ffn_norm_g": 1.0 + nrm(ks[7], (DEPTH, D), 0.02),
        "ab_w_in": nrm(ks[8], (N_EVEN, D, 3 * D_A + 2 * D_B), D ** -0.5),
        "ab_conv_a": nrm(ks[9], (N_EVEN, K_A, D_A), K_A ** -0.5),
        "ab_conv_b_w": nrm(ks[10], (N_EVEN, K_B, D_B), K_B ** -0.5),
        "ab_conv_b_b": nrm(ks[11], (N_EVEN, D_B), 0.02),
        "ab_ln_g": 1.0 + nrm(ks[12], (N_EVEN, D_B), 0.02),
        "ab_ln_b": nrm(ks[13], (N_EVEN, D_B), 0.02),
        "ab_w_out": nrm(ks[14], (N_EVEN, D_A + D_B, D), (D_A + D_B) ** -0.5),
        "cd_w_in": nrm(ks[15], (N_ODD, D, 2 * D_C + D_D), D ** -0.5),
        "cd_ln_g": 1.0 + nrm(ks[16], (N_ODD, D_C), 0.02),
        "cd_ln_b": nrm(ks[17], (N_ODD, D_C), 0.02),
        "cd_w_s": nrm(ks[18], (N_ODD, H_C, CHUNK, CHUNK), CHUNK ** -0.5),
        "cd_b_s": nrm(ks[19], (N_ODD, H_C, CHUNK), 0.02),
        "cd_w_out": nrm(ks[20], (N_ODD, D_C + D_D, D), (D_C + D_D) ** -0.5),
        "ffn_w_in": nrm(ks[21], (DEPTH, D, 2 * D_FF), D ** -0.5),
        "ffn_w_out": nrm(ks[22], (DEPTH, D_FF, D), D_FF ** -0.5),
        "final_g": 1.0 + nrm(ks[23], (D,), 0.02),
    }


def reference(x_prompt, x_sample, c_prompt, c_sample, ada_w, ada_b, mix_norm_g, ffn_norm_g,
              ab_w_in, ab_conv_a, ab_conv_b_w, ab_conv_b_b, ab_ln_g, ab_ln_b, ab_w_out,
              cd_w_in, cd_ln_g, cd_ln_b, cd_w_s, cd_b_s, cd_w_out,
              ffn_w_in, ffn_w_out, final_g):
    y_prompt = trunk(x_prompt, c_prompt, ada_w, ada_b, mix_norm_g, ffn_norm_g,
                     ab_w_in, ab_conv_a, ab_conv_b_w, ab_conv_b_b, ab_ln_g, ab_ln_b, ab_w_out,
                     cd_w_in, cd_ln_g, cd_ln_b, cd_w_s, cd_b_s, cd_w_out,
                     ffn_w_in, ffn_w_out, final_g)
    y_sample = trunk(x_sample, c_sample, ada_w, ada_b, mix_norm_g, ffn_norm_g,
                     ab_w_in, ab_conv_a, ab_conv_b_w, ab_conv_b_b, ab_ln_g, ab_ln_b, ab_w_out,
                     cd_w_in, cd_ln_g, cd_ln_b, cd_w_s, cd_b_s, cd_w_out,
                     ffn_w_in, ffn_w_out, final_g)
    return (y_prompt, y_sample)
```

```python
import functools

import jax
import jax.numpy as jnp
import numpy as np
from jax import lax
from jax.experimental import pallas as pl
from jax.experimental.pallas import tpu as pltpu

D_MODEL = 1024
D_A = 512
D_B = 512
K_A = 3
K_B = 31
D_C = 768
H_C = 6
CHUNK = 128
D_D = 256
DG_D = 64
D_FF = 2816
N_MOD = 6
EPS = 1e-6

LANES = 128
SUBLANES = 8
TOKEN_TILE = 512
HALO = 16
FFN_CHUNKS = 2
DFT_N2 = 128
MOD_ROWS = 16
MOD_COLS = 1536
VMEM_LIMIT = 56 * 1024 * 1024

_BF16 = jnp.bfloat16
_F32 = jnp.float32


def _const_spec(shape):
    nd = len(shape)
    return pl.BlockSpec(shape, lambda *_: (0,) * nd, pipeline_mode=pl.Buffered(1))


def _params(n_grid):
    return pltpu.CompilerParams(
        dimension_semantics=("parallel",) * n_grid, vmem_limit_bytes=VMEM_LIMIT)


def _dot(a, b):
    return jnp.dot(a, b, preferred_element_type=_F32)


def _modulated_rmsnorm(x, gain, scale, shift):
    ms = jnp.mean(x * x, axis=-1, keepdims=True)
    return x * lax.rsqrt(ms + EPS) * (gain * (1.0 + scale)) + shift


def _layernorm(x, g, b):
    mu = jnp.mean(x, axis=-1, keepdims=True)
    xc = x - mu
    var = jnp.mean(xc * xc, axis=-1, keepdims=True)
    return xc * lax.rsqrt(var + EPS) * g + b


def _mod_slices(mod):
    d = D_MODEL
    return [mod[:, i * d:(i + 1) * d] for i in range(N_MOD)]


def _mod_body(c_ref, w_ref, b_ref, o_ref):
    c = c_ref[...]
    sc = (c * jax.nn.sigmoid(c)).astype(_BF16)
    o_ref[...] = _dot(sc, w_ref[...].astype(_BF16)) + b_ref[...]


def _mod_call(c_all, ada_w, ada_b):
    depth = ada_w.shape[0]
    n_out = ada_w.shape[2]
    return pl.pallas_call(
        _mod_body,
        out_shape=jax.ShapeDtypeStruct((depth, MOD_ROWS, n_out), _F32),
        grid=(depth, n_out // MOD_COLS),
        in_specs=[
            pl.BlockSpec((MOD_ROWS, D_MODEL), lambda l, n: (0, 0)),
            pl.BlockSpec((None, D_MODEL, MOD_COLS), lambda l, n: (l, 0, n)),
            pl.BlockSpec((None, 1, MOD_COLS), lambda l, n: (l, 0, n)),
        ],
        out_specs=pl.BlockSpec((None, MOD_ROWS, MOD_COLS), lambda l, n: (l, 0, n)),
        compiler_params=_params(2),
        name="adaln_mod",
    )(c_all, ada_w, ada_b.reshape(depth, 1, n_out))


def _mix_ab_body(xp_ref, x_ref, xn_ref, mod_ref, ng_ref, win_ref, ca_ref, cbw_ref, cbb_ref,
                 lng_ref, lnb_ref, wout_ref, o_ref, u_scr, g_scr):
    t = pl.program_id(1)
    nt = pl.num_programs(1)
    tile = x_ref.shape[0]
    rows = tile + 2 * HALO
    sh1, sc1, g1, _, _, _ = _mod_slices(mod_ref[...])

    xin = jnp.concatenate([xp_ref[...], x_ref[...], xn_ref[...]], axis=0)
    h = _modulated_rmsnorm(xin, ng_ref[...], sc1, sh1).astype(_BF16)
    p = _dot(h, win_ref[...])

    r = lax.broadcasted_iota(jnp.int32, (rows, 1), 0)
    valid = jnp.logical_and(jnp.logical_or(r >= HALO, t > 0),
                            jnp.logical_or(r < HALO + tile, t < nt - 1))
    a_b = p[HALO:HALO + tile, 0:D_A]
    u_scr[...] = jnp.where(valid, p[:, D_A:2 * D_A] * p[:, 2 * D_A:3 * D_A], 0.0)
    b_val = p[:, 3 * D_A:3 * D_A + D_B]
    b_gate = p[:, 3 * D_A + D_B:]
    g_scr[...] = jnp.where(valid, b_val * jax.nn.sigmoid(b_gate), 0.0)

    conv_a = jnp.zeros((tile, D_A), _F32)
    for k in range(K_A):
        conv_a = conv_a + u_scr[pl.ds(HALO - K_A // 2 + k, tile), :] * ca_ref[k:k + 1, :]
    y_a = a_b * conv_a

    conv_b = jnp.zeros((tile, D_B), _F32)
    for k in range(K_B):
        conv_b = conv_b + g_scr[pl.ds(HALO - K_B // 2 + k, tile), :] * cbw_ref[k:k + 1, :]
    z = _layernorm(conv_b + cbb_ref[...], lng_ref[...], lnb_ref[...])
    y_b = z * jax.nn.sigmoid(z)

    m = _dot(y_a.astype(_BF16), wout_ref[0:D_A, :]) + _dot(y_b.astype(_BF16), wout_ref[D_A:, :])
    o_ref[...] = x_ref[...] + g1 * m


def _mix_ab_call(x, mod, b_off, ng, w_in, conv_a, conv_b_w, conv_b_b, ln_g, ln_b, w_out):
    bsz, seq, d = x.shape
    tile = TOKEN_TILE
    nt = seq // tile
    per = tile // HALO
    n_halo_blocks = seq // HALO
    n_in = w_in.shape[1]
    return pl.pallas_call(
        _mix_ab_body,
        out_shape=jax.ShapeDtypeStruct(x.shape, _F32),
        grid=(bsz, nt),
        in_specs=[
            pl.BlockSpec((None, HALO, d), lambda b, t: (b, jnp.maximum(t * per - 1, 0), 0)),
            pl.BlockSpec((None, tile, d), lambda b, t: (b, t, 0)),
            pl.BlockSpec((None, HALO, d), lambda b, t: (b, jnp.minimum((t + 1) * per, n_halo_blocks - 1), 0)),
            pl.BlockSpec((None, 1, N_MOD * d), lambda b, t: (b + b_off, 0, 0)),
            _const_spec((1, d)),
            _const_spec((d, n_in)),
            _const_spec((K_A, D_A)),
            _const_spec((K_B, D_B)),
            _const_spec((1, D_B)),
            _const_spec((1, D_B)),
            _const_spec((1, D_B)),
            _const_spec((D_A + D_B, d)),
        ],
        out_specs=pl.BlockSpec((None, tile, d), lambda b, t: (b, t, 0)),
        scratch_shapes=[pltpu.VMEM((tile + 2 * HALO, D_A), _F32),
                        pltpu.VMEM((tile + 2 * HALO, D_B), _F32)],
        compiler_params=_params(2),
        name="mixer_ab",
    )(x, x, x, mod, ng, w_in, conv_a, conv_b_w, conv_b_b, ln_g, ln_b, w_out)


def _ffn_core(x, mod, ng_ref, win_ref, wout_ref):
    _, _, _, sh2, sc2, g2 = _mod_slices(mod)
    h = _modulated_rmsnorm(x, ng_ref[...], sc2, sh2).astype(_BF16)
    fc = D_FF // FFN_CHUNKS
    acc = jnp.zeros(x.shape, _F32)
    for c in range(FFN_CHUNKS):
        gate = _dot(h, win_ref[:, c * fc:(c + 1) * fc])
        up = _dot(h, win_ref[:, D_FF + c * fc:D_FF + (c + 1) * fc])
        a = (gate * jax.nn.sigmoid(gate) * up).astype(_BF16)
        acc = acc + _dot(a, wout_ref[c * fc:(c + 1) * fc, :])
    return x + g2 * acc


def _ffn_body(x_ref, mod_ref, ng_ref, win_ref, wout_ref, o_ref):
    o_ref[...] = _ffn_core(x_ref[...], mod_ref[...], ng_ref, win_ref, wout_ref)


def _ffn_final_body(x_ref, yd_ref, mod_ref, ng_ref, win_ref, wout_ref, wd_ref, fg_ref, o_ref):
    mod = mod_ref[...]
    g1 = mod[:, 2 * D_MODEL:3 * D_MODEL]
    yd = jnp.concatenate([yd_ref[0], yd_ref[1]], axis=1).astype(_BF16)
    x = x_ref[...] + g1 * _dot(yd, wd_ref[...])
    x = _ffn_core(x, mod, ng_ref, win_ref, wout_ref)
    ms = jnp.mean(x * x, axis=-1, keepdims=True)
    o_ref[...] = x * lax.rsqrt(ms + EPS) * fg_ref[...]


def _ffn_call(x, mod, b_off, ng, w_in, w_out):
    bsz, seq, d = x.shape
    tile = TOKEN_TILE
    return pl.pallas_call(
        _ffn_body,
        out_shape=jax.ShapeDtypeStruct(x.shape, _F32),
        grid=(bsz, seq // tile),
        in_specs=[
            pl.BlockSpec((None, tile, d), lambda b, t: (b, t, 0)),
            pl.BlockSpec((None, 1, N_MOD * d), lambda b, t: (b + b_off, 0, 0)),
            _const_spec((1, d)),
            _const_spec((d, 2 * D_FF)),
            _const_spec((D_FF, d)),
        ],
        out_specs=pl.BlockSpec((None, tile, d), lambda b, t: (b, t, 0)),
        compiler_params=_params(2),
        name="ffn",
    )(x, mod, ng, w_in, w_out)


def _ffn_final_call(x, yd, mod, b_off, ng, w_in, w_out, w_d, final_g):
    bsz, seq, d = x.shape
    tile = TOKEN_TILE
    return pl.pallas_call(
        _ffn_final_body,
        out_shape=jax.ShapeDtypeStruct(x.shape, _F32),
        grid=(bsz, seq // tile),
        in_specs=[
            pl.BlockSpec((None, tile, d), lambda b, t: (b, t, 0)),
            pl.BlockSpec((None, 2, tile, LANES), lambda b, t: (b, 0, t, 0)),
            pl.BlockSpec((None, 1, N_MOD * d), lambda b, t: (b + b_off, 0, 0)),
            _const_spec((1, d)),
            _const_spec((d, 2 * D_FF)),
            _const_spec((D_FF, d)),
            _const_spec((D_D, d)),
            _const_spec((1, d)),
        ],
        out_specs=pl.BlockSpec((None, tile, d), lambda b, t: (b, t, 0)),
        compiler_params=_params(2),
        name="ffn_final",
    )(x, yd, mod, ng, w_in, w_out, w_d, final_g)


def _mix_cd_body(x_ref, mod_ref, ng_ref, win_ref, lng_ref, lnb_ref, ws_ref, bs_ref, woutc_ref,
                 cdft_ref, o_ref, z_ref):
    tile = x_ref.shape[0]
    n_chunks = tile // CHUNK
    x = x_ref[...]
    sh1, sc1, g1, _, _, _ = _mod_slices(mod_ref[...])
    h = _modulated_rmsnorm(x, ng_ref[...], sc1, sh1).astype(_BF16)
    p = _dot(h, win_ref[...])
    u = p[:, 0:D_C]
    v = _layernorm(p[:, D_C:2 * D_C], lng_ref[...], lnb_ref[...]).astype(_BF16)
    f = p[:, 2 * D_C:].astype(_BF16)

    head_cols = []
    for hd in range(H_C):
        lo = hd * CHUNK
        rhs = jnp.concatenate([v[n * CHUNK:(n + 1) * CHUNK, lo:lo + CHUNK] for n in range(n_chunks)], axis=1)
        sv = _dot(ws_ref[hd], rhs)
        bias = bs_ref[hd]
        head_cols.append(jnp.concatenate(
            [sv[:, n * CHUNK:(n + 1) * CHUNK] + bias for n in range(n_chunks)], axis=0))
    y_c = (u * jnp.concatenate(head_cols, axis=1)).astype(_BF16)
    o_ref[...] = x + g1 * _dot(y_c, woutc_ref[...])

    z = _dot(f, cdft_ref[...])
    for part in range(2):
        for half in range(2):
            lo = (2 * part + half) * LANES
            z_ref[part, half] = z[:, lo:lo + LANES]


def _mix_cd_call(x, mod, b_off, ng, w_in, ln_g, ln_b, w_s, b_s_rows, w_out_c, cdft):
    bsz, seq, d = x.shape
    tile = TOKEN_TILE
    n_in = w_in.shape[1]
    return pl.pallas_call(
        _mix_cd_body,
        out_shape=(jax.ShapeDtypeStruct(x.shape, _F32),
                   jax.ShapeDtypeStruct((bsz, 2, 2, seq, LANES), _F32)),
        grid=(bsz, seq // tile),
        in_specs=[
            pl.BlockSpec((None, tile, d), lambda b, t: (b, t, 0)),
            pl.BlockSpec((None, 1, N_MOD * d), lambda b, t: (b + b_off, 0, 0)),
            _const_spec((1, d)),
            _const_spec((d, n_in)),
            _const_spec((1, D_C)),
            _const_spec((1, D_C)),
            _const_spec((H_C, CHUNK, CHUNK)),
            _const_spec((H_C, CHUNK, CHUNK)),
            _const_spec((D_C, d)),
            _const_spec((D_D, 2 * D_D)),
        ],
        out_specs=(pl.BlockSpec((None, tile, d), lambda b, t: (b, t, 0)),
                   pl.BlockSpec((None, 2, 2, tile, LANES), lambda b, t: (b, 0, 0, t, 0))),
        compiler_params=_params(2),
        name="mixer_cd",
    )(x, mod, ng, w_in, ln_g, ln_b, w_s, b_s_rows, w_out_c, cdft)


def _dft1_body(z_ref, w1_ref, a_ref):
    n1 = z_ref.shape[2]
    cols = []
    gathered = [[pltpu.einshape("abc->bac", z_ref[part, half]) for half in range(2)] for part in range(2)]
    for i in range(SUBLANES):
        for half in range(2):
            cols.append(jnp.concatenate([gathered[0][half][i], gathered[1][half][i]], axis=0))
    rhs = jnp.concatenate(cols, axis=1).astype(_BF16)
    a = _dot(w1_ref[...], rhs)
    for part in range(2):
        for half in range(2):
            blk = jnp.stack([a[part * n1:(part + 1) * n1, (2 * i + half) * LANES:(2 * i + half + 1) * LANES]
                             for i in range(SUBLANES)], axis=0)
            a_ref[part, half] = pltpu.einshape("bac->abc", blk)


def _dft1_call(z, w1):
    bsz, _, _, seq, _ = z.shape
    n2 = DFT_N2
    n1 = seq // n2
    z6 = z.reshape(bsz, 2, 2, n1, n2, LANES)
    spec = pl.BlockSpec((None, 2, 2, n1, SUBLANES, LANES), lambda b, j: (b, 0, 0, 0, j, 0))
    return pl.pallas_call(
        _dft1_body,
        out_shape=jax.ShapeDtypeStruct(z6.shape, _F32),
        grid=(bsz, n2 // SUBLANES),
        in_specs=[spec, _const_spec((2 * n1, 2 * n1))],
        out_specs=spec,
        compiler_params=_params(2),
        name="seq_dft_stage1",
    )(z6, w1)


def _dft2_body(a_ref, m_ref, y_ref):
    outs = []
    for i in range(SUBLANES):
        re = jnp.concatenate([a_ref[0, 0, i], a_ref[0, 1, i]], axis=1)
        im = jnp.concatenate([a_ref[1, 0, i], a_ref[1, 1, i]], axis=1)
        rhs = jnp.concatenate([re, im], axis=0).astype(_BF16)
        outs.append(_dot(m_ref[i], rhs))
    for half in range(2):
        blk = jnp.stack([o[:, half * LANES:(half + 1) * LANES] for o in outs], axis=0)
        y_ref[half] = pltpu.einshape("bac->abc", blk)


def _dft2_call(a6, m2):
    bsz, _, _, n1, n2, _ = a6.shape
    seq = n1 * n2
    y = pl.pallas_call(
        _dft2_body,
        out_shape=jax.ShapeDtypeStruct((bsz, 2, n2, n1, LANES), _F32),
        grid=(bsz, n1 // SUBLANES),
        in_specs=[
            pl.BlockSpec((None, 2, 2, SUBLANES, n2, LANES), lambda b, j: (b, 0, 0, j, 0, 0)),
            pl.BlockSpec((SUBLANES, n2, 2 * n2), lambda b, j: (j, 0, 0)),
        ],
        out_specs=pl.BlockSpec((None, 2, n2, SUBLANES, LANES), lambda b, j: (b, 0, 0, j, 0)),
        compiler_params=_params(2),
        name="seq_dft_stage2",
    )(a6, m2)
    return y.reshape(bsz, 2, seq, LANES)


def _channel_dft_matrix():
    c = np.arange(D_D)
    same = (c[:, None] // DG_D) == (c[None, :] // DG_D)
    ang = 2.0 * np.pi * ((c[:, None] % DG_D) * (c[None, :] % DG_D) % DG_D) / DG_D
    scale = 1.0 / np.sqrt(DG_D)
    re = np.where(same, np.cos(ang), 0.0) * scale
    im = np.where(same, -np.sin(ang), 0.0) * scale
    return np.concatenate([re, im], axis=1)


def _stage1_matrix(n1):
    k = np.arange(n1)
    ang = 2.0 * np.pi * ((k[:, None] * k[None, :]) % n1) / n1
    c, s = np.cos(ang), np.sin(ang)
    return np.block([[c, s], [-s, c]])


def _stage2_matrices(n1, n2):
    seq = n1 * n2
    k1 = np.arange(n1)[:, None, None]
    k2 = np.arange(n2)[None, :, None]
    nn = np.arange(n2)[None, None, :]
    ang = 2.0 * np.pi * ((nn * (k1 + n1 * k2)) % seq) / seq
    scale = 1.0 / np.sqrt(seq)
    return np.concatenate([np.cos(ang), np.sin(ang)], axis=2) * scale


def _trunk(x, b_off, mod, wts):
    seq = x.shape[1]
    n1 = seq // DFT_N2
    x = _mix_ab_call(x, mod[0], b_off, wts["mix_g"][0], wts["ab_w_in"], wts["ab_conv_a"], wts["ab_conv_b_w"],
                     wts["ab_conv_b_b"], wts["ab_ln_g"], wts["ab_ln_b"], wts["ab_w_out"])
    x = _ffn_call(x, mod[0], b_off, wts["ffn_g"][0], wts["ffn_w_in"][0], wts["ffn_w_out"][0])
    x, z = _mix_cd_call(x, mod[1], b_off, wts["mix_g"][1], wts["cd_w_in"], wts["cd_ln_g"], wts["cd_ln_b"],
                        wts["cd_w_s"], wts["cd_b_s_rows"], wts["cd_w_out_c"], wts["cdft"])
    a = _dft1_call(z, jnp.asarray(_stage1_matrix(n1), _BF16))
    yd = _dft2_call(a, jnp.asarray(_stage2_matrices(n1, DFT_N2), _BF16))
    return _ffn_final_call(x, yd, mod[1], b_off, wts["ffn_g"][1], wts["ffn_w_in"][1], wts["ffn_w_out"][1],
                           wts["cd_w_out_d"], wts["final_g"])


def kernel(x_prompt, x_sample, c_prompt, c_sample, ada_w, ada_b, mix_norm_g, ffn_norm_g, ab_w_in, ab_conv_a,
           ab_conv_b_w, ab_conv_b_b, ab_ln_g, ab_ln_b, ab_w_out, cd_w_in, cd_ln_g, cd_ln_b, cd_w_s, cd_b_s,
           cd_w_out, ffn_w_in, ffn_w_out, final_g):
    depth = ada_w.shape[0]
    assert depth == 2 and ab_w_in.shape[0] == 1 and cd_w_in.shape[0] == 1
    n_prompt, n_sample = c_prompt.shape[0], c_sample.shape[0]
    assert n_prompt + n_sample <= MOD_ROWS

    c_all = jnp.concatenate(
        [c_prompt, c_sample, jnp.zeros((MOD_ROWS - n_prompt - n_sample, D_MODEL), _F32)], axis=0)
    mod = _mod_call(c_all, ada_w, ada_b).reshape(depth, MOD_ROWS, 1, N_MOD * D_MODEL)

    wts = {
        "mix_g": mix_norm_g.reshape(depth, 1, D_MODEL),
        "ffn_g": ffn_norm_g.reshape(depth, 1, D_MODEL),
        "ab_w_in": ab_w_in[0].astype(_BF16),
        "ab_conv_a": ab_conv_a[0],
        "ab_conv_b_w": ab_conv_b_w[0],
        "ab_conv_b_b": ab_conv_b_b,
        "ab_ln_g": ab_ln_g,
        "ab_ln_b": ab_ln_b,
        "ab_w_out": ab_w_out[0].astype(_BF16),
        "cd_w_in": cd_w_in[0].astype(_BF16),
        "cd_ln_g": cd_ln_g,
        "cd_ln_b": cd_ln_b,
        "cd_w_s": cd_w_s[0].astype(_BF16),
        "cd_b_s_rows": jnp.broadcast_to(cd_b_s[0][:, :, None], (H_C, CHUNK, CHUNK)),
        "cd_w_out_c": cd_w_out[0, :D_C].astype(_BF16),
        "cd_w_out_d": cd_w_out[0, D_C:].astype(_BF16),
        "cdft": jnp.asarray(_channel_dft_matrix(), _BF16),
        "ffn_w_in": ffn_w_in.astype(_BF16),
        "ffn_w_out": ffn_w_out.astype(_BF16),
        "final_g": final_g.reshape(1, D_MODEL),
    }
    y_prompt = _trunk(x_prompt, 0, mod, wts)
    y_sample = _trunk(x_sample, n_prompt, mod, wts)
    return (y_prompt, y_sample)
```

```python
import jax
import jax.numpy as jnp
import numpy as np
from jax import lax
from jax.experimental import pallas as pl
from jax.experimental.pallas import tpu as pltpu

D_MODEL = 1024
D_A = 512
D_B = 512
K_A = 3
K_B = 31
D_C = 768
H_C = 6
CHUNK = 128
D_D = 256
DG_D = 64
D_FF = 2816
N_MOD = 6
EPS = 1e-6

LANES = 128
SUBLANES = 8
MXU_DIM = 256
TOKEN_TILE = 512
SUB_TILES = 2
HALO = 16
CONV_STRIDE = 4
CONV_GROUP = SUBLANES * CONV_STRIDE
CONV_ACCS = 8
FFN_SPLIT = 6 * MXU_DIM
DFT_N2 = 128
MOD_ROWS = 16
MOD_COLS = 1536
VMEM_LIMIT = 56 * 1024 * 1024

_BF16 = jnp.bfloat16
_F32 = jnp.float32


def _const_spec(shape):
    nd = len(shape)
    return pl.BlockSpec(shape, lambda *_: (0,) * nd, pipeline_mode=pl.Buffered(1))


def _params(n_grid):
    return pltpu.CompilerParams(
        dimension_semantics=("parallel",) * n_grid, vmem_limit_bytes=VMEM_LIMIT)


def _dot(a, b):
    return jnp.dot(a, b, preferred_element_type=_F32)


def _modulated_rmsnorm(x, gain, scale, shift):
    ms = jnp.mean(x * x, axis=-1, keepdims=True)
    return x * lax.rsqrt(ms + EPS) * (gain * (1.0 + scale)) + shift


def _layernorm(x, g, b):
    mu = jnp.mean(x, axis=-1, keepdims=True)
    xc = x - mu
    var = jnp.mean(xc * xc, axis=-1, keepdims=True)
    return xc * lax.rsqrt(var + EPS) * g + b


def _mod_slices(mod):
    d = D_MODEL
    return [mod[:, i * d:(i + 1) * d] for i in range(N_MOD)]


def _swiglu_residual(x, mod, ng_ref, win_ref, wout_ref):
    _, _, _, sh2, sc2, g2 = _mod_slices(mod)
    h = _modulated_rmsnorm(x, ng_ref[...], sc2, sh2).astype(_BF16)
    acc = None
    for lo, hi in ((0, FFN_SPLIT), (FFN_SPLIT, D_FF)):
        gate = _dot(h, win_ref[:, lo:hi])
        up = _dot(h, win_ref[:, D_FF + lo:D_FF + hi])
        a = (gate * jax.nn.sigmoid(gate) * up).astype(_BF16)
        part = _dot(a, wout_ref[lo:hi, :])
        acc = part if acc is None else acc + part
    return x + g2 * acc


def _mod_body(c_ref, w_ref, b_ref, o_ref):
    c = c_ref[...]
    sc = (c * jax.nn.sigmoid(c)).astype(_BF16)
    o_ref[...] = _dot(sc, w_ref[...].astype(_BF16)) + b_ref[...]


def _mod_call(c_all, ada_w, ada_b):
    depth = ada_w.shape[0]
    n_out = ada_w.shape[2]
    return pl.pallas_call(
        _mod_body,
        out_shape=jax.ShapeDtypeStruct((depth, MOD_ROWS, n_out), _F32),
        grid=(depth, n_out // MOD_COLS),
        in_specs=[
            pl.BlockSpec((MOD_ROWS, D_MODEL), lambda l, n: (0, 0)),
            pl.BlockSpec((None, D_MODEL, MOD_COLS), lambda l, n: (l, 0, n)),
            pl.BlockSpec((None, 1, MOD_COLS), lambda l, n: (l, 0, n)),
        ],
        out_specs=pl.BlockSpec((None, MOD_ROWS, MOD_COLS), lambda l, n: (l, 0, n)),
        compiler_params=_params(2),
        name="adaln_mod",
    )(c_all, ada_w, ada_b.reshape(depth, 1, n_out))


def _dwconv(src_ref, w_ref, first_row, n_tokens, dst_ref, slabs):
    n_taps = w_ref.shape[0]
    n_vregs = n_tokens // SUBLANES
    for c in slabs:
        lanes = slice(c * LANES, (c + 1) * LANES)
        for v0 in range(0, n_vregs, CONV_ACCS):
            starts = [CONV_GROUP * (v // CONV_STRIDE) + v % CONV_STRIDE for v in range(v0, v0 + CONV_ACCS)]
            accs = [None] * CONV_ACCS
            for k in range(n_taps):
                w = w_ref[k, :, lanes]
                for q, t0 in enumerate(starts):
                    term = w * src_ref[c, pl.ds(first_row + k + t0, SUBLANES, stride=CONV_STRIDE), :]
                    accs[q] = term if accs[q] is None else accs[q] + term
            for q, t0 in enumerate(starts):
                dst_ref[c, pl.ds(t0, SUBLANES, stride=CONV_STRIDE), :] = accs[q]


def _slabs(ref):
    return jnp.concatenate([ref[c] for c in range(ref.shape[0])], axis=1)


def _layer0_body(xp_ref, x_ref, xn_ref, mod_ref, ng_ref, win_ref, ca_ref, cbw_ref, cbb_ref,
                 lng_ref, lnb_ref, wout_ref, fg_ref, fwin_ref, fwout_ref, o_ref,
                 u_scr, g_scr, ca_scr, cb_scr):
    t = pl.program_id(1)
    nt = pl.num_programs(1)
    tile = x_ref.shape[0]
    sub = tile // SUB_TILES
    rows = sub + 2 * HALO
    half = MXU_DIM
    slabs_per_half = half // LANES
    mod = mod_ref[...]
    sh1, sc1, g1, _, _, _ = _mod_slices(mod)

    xin = jnp.concatenate([xp_ref[...], x_ref[...], xn_ref[...]], axis=0)
    h = _modulated_rmsnorm(xin, ng_ref[...], sc1, sh1).astype(_BF16)
    for s in range(SUB_TILES):
        hs = h[s * sub:s * sub + rows, :]
        r = lax.broadcasted_iota(jnp.int32, (rows, 1), 0) + s * sub
        valid = jnp.logical_and(jnp.logical_or(r >= HALO, t > 0),
                                jnp.logical_or(r < HALO + tile, t < nt - 1))

        for j in range(D_B // half):
            q = _dot(hs, win_ref[:, 2 * half * j:2 * half * (j + 1)])
            g = jnp.where(valid, q[:, :half] * jax.nn.sigmoid(q[:, half:]), 0.0)
            slabs = range(slabs_per_half * j, slabs_per_half * (j + 1))
            for c in slabs:
                lo = (c - slabs[0]) * LANES
                g_scr[s, c] = g[:, lo:lo + LANES]
            _dwconv(g_scr.at[s], cbw_ref, HALO - K_B // 2, sub, cb_scr.at[s], slabs)
        for j in range(D_A // half):
            base = 2 * D_B + 2 * half * j
            q = _dot(hs, win_ref[:, base:base + 2 * half])
            u = jnp.where(valid, q[:, :half] * q[:, half:], 0.0)
            slabs = range(slabs_per_half * j, slabs_per_half * (j + 1))
            for c in slabs:
                lo = (c - slabs[0]) * LANES
                u_scr[s, c] = u[:, lo:lo + LANES]
            _dwconv(u_scr.at[s], ca_ref, HALO - K_A // 2, sub, ca_scr.at[s], slabs)
        a_b = _dot(hs[HALO:HALO + sub, :], win_ref[:, 2 * D_B + 2 * D_A:])

        y_a = a_b * _slabs(ca_scr.at[s])
        z = _layernorm(_slabs(cb_scr.at[s]) + cbb_ref[...], lng_ref[...], lnb_ref[...])
        y_b = z * jax.nn.sigmoid(z)

        m = _dot(y_a.astype(_BF16), wout_ref[0:D_A, :]) + _dot(y_b.astype(_BF16), wout_ref[D_A:, :])
        x1 = x_ref[s * sub:(s + 1) * sub, :] + g1 * m
        o_ref[s * sub:(s + 1) * sub, :] = _swiglu_residual(x1, mod, fg_ref, fwin_ref, fwout_ref)


def _layer0_call(x, mod, b_off, ng, w_in, conv_a, conv_b_w, conv_b_b, ln_g, ln_b, w_out, fg, f_w_in, f_w_out):
    bsz, seq, d = x.shape
    tile = TOKEN_TILE
    sub = tile // SUB_TILES
    per = tile // HALO
    n_halo_blocks = seq // HALO
    n_in = w_in.shape[1]
    slab_rows = sub + 2 * HALO
    return pl.pallas_call(
        _layer0_body,
        out_shape=jax.ShapeDtypeStruct(x.shape, _F32),
        grid=(bsz, seq // tile),
        in_specs=[
            pl.BlockSpec((None, HALO, d), lambda b, t: (b, jnp.maximum(t * per - 1, 0), 0)),
            pl.BlockSpec((None, tile, d), lambda b, t: (b, t, 0)),
            pl.BlockSpec((None, HALO, d), lambda b, t: (b, jnp.minimum((t + 1) * per, n_halo_blocks - 1), 0)),
            pl.BlockSpec((None, 1, N_MOD * d), lambda b, t: (b + b_off, 0, 0)),
            _const_spec((1, d)),
            _const_spec((d, n_in)),
            _const_spec((K_A, SUBLANES, D_A)),
            _const_spec((K_B, SUBLANES, D_B)),
            _const_spec((1, D_B)),
            _const_spec((1, D_B)),
            _const_spec((1, D_B)),
            _const_spec((D_A + D_B, d)),
            _const_spec((1, d)),
            _const_spec((d, 2 * D_FF)),
            _const_spec((D_FF, d)),
        ],
        out_specs=pl.BlockSpec((None, tile, d), lambda b, t: (b, t, 0)),
        scratch_shapes=[pltpu.VMEM((SUB_TILES, D_A // LANES, slab_rows, LANES), _F32),
                        pltpu.VMEM((SUB_TILES, D_B // LANES, slab_rows, LANES), _F32),
                        pltpu.VMEM((SUB_TILES, D_A // LANES, sub, LANES), _F32),
                        pltpu.VMEM((SUB_TILES, D_B // LANES, sub, LANES), _F32)],
        compiler_params=_params(2),
        name="layer0",
    )(x, x, x, mod, ng, w_in, conv_a, conv_b_w, conv_b_b, ln_g, ln_b, w_out, fg, f_w_in, f_w_out)


def _ffn_final_body(x_ref, yd_ref, mod_ref, ng_ref, win_ref, wout_ref, wd_ref, fg_ref, o_ref):
    mod = mod_ref[...]
    g1 = mod[:, 2 * D_MODEL:3 * D_MODEL]
    yd = jnp.concatenate([yd_ref[0], yd_ref[1]], axis=1).astype(_BF16)
    x = x_ref[...] + g1 * _dot(yd, wd_ref[...])
    x = _swiglu_residual(x, mod, ng_ref, win_ref, wout_ref)
    ms = jnp.mean(x * x, axis=-1, keepdims=True)
    o_ref[...] = x * lax.rsqrt(ms + EPS) * fg_ref[...]


def _ffn_final_call(x, yd, mod, b_off, ng, w_in, w_out, w_d, final_g):
    bsz, seq, d = x.shape
    tile = TOKEN_TILE
    return pl.pallas_call(
        _ffn_final_body,
        out_shape=jax.ShapeDtypeStruct(x.shape, _F32),
        grid=(bsz, seq // tile),
        in_specs=[
            pl.BlockSpec((None, tile, d), lambda b, t: (b, t, 0)),
            pl.BlockSpec((None, 2, tile, LANES), lambda b, t: (b, 0, t, 0)),
            pl.BlockSpec((None, 1, N_MOD * d), lambda b, t: (b + b_off, 0, 0)),
            _const_spec((1, d)),
            _const_spec((d, 2 * D_FF)),
            _const_spec((D_FF, d)),
            _const_spec((D_D, d)),
            _const_spec((1, d)),
        ],
        out_specs=pl.BlockSpec((None, tile, d), lambda b, t: (b, t, 0)),
        compiler_params=_params(2),
        name="ffn_final",
    )(x, yd, mod, ng, w_in, w_out, w_d, final_g)


def _mix_cd_body(x_ref, mod_ref, ng_ref, win_ref, lng_ref, lnb_ref, ws_ref, bs_ref, woutc_ref,
                 cdft_ref, o_ref, z_ref):
    tile = x_ref.shape[0]
    n_chunks = tile // CHUNK
    x = x_ref[...]
    sh1, sc1, g1, _, _, _ = _mod_slices(mod_ref[...])
    h = _modulated_rmsnorm(x, ng_ref[...], sc1, sh1).astype(_BF16)
    p = _dot(h, win_ref[...])
    u = p[:, 0:D_C]
    v = _layernorm(p[:, D_C:2 * D_C], lng_ref[...], lnb_ref[...]).astype(_BF16)
    f = p[:, 2 * D_C:].astype(_BF16)

    head_cols = []
    for hd in range(H_C):
        lo = hd * CHUNK
        rhs = jnp.concatenate([v[n * CHUNK:(n + 1) * CHUNK, lo:lo + CHUNK] for n in range(n_chunks)], axis=1)
        sv = _dot(ws_ref[hd], rhs)
        bias = bs_ref[hd]
        head_cols.append(jnp.concatenate(
            [sv[:, n * CHUNK:(n + 1) * CHUNK] + bias for n in range(n_chunks)], axis=0))
    y_c = (u * jnp.concatenate(head_cols, axis=1)).astype(_BF16)
    o_ref[...] = x + g1 * _dot(y_c, woutc_ref[...])

    z = _dot(f, cdft_ref[...])
    for part in range(2):
        for half in range(2):
            lo = (2 * part + half) * LANES
            z_ref[part, half] = z[:, lo:lo + LANES]


def _mix_cd_call(x, mod, b_off, ng, w_in, ln_g, ln_b, w_s, b_s_rows, w_out_c, cdft):
    bsz, seq, d = x.shape
    tile = TOKEN_TILE
    n_in = w_in.shape[1]
    return pl.pallas_call(
        _mix_cd_body,
        out_shape=(jax.ShapeDtypeStruct(x.shape, _F32),
                   jax.ShapeDtypeStruct((bsz, 2, 2, seq, LANES), _F32)),
        grid=(bsz, seq // tile),
        in_specs=[
            pl.BlockSpec((None, tile, d), lambda b, t: (b, t, 0)),
            pl.BlockSpec((None, 1, N_MOD * d), lambda b, t: (b + b_off, 0, 0)),
            _const_spec((1, d)),
            _const_spec((d, n_in)),
            _const_spec((1, D_C)),
            _const_spec((1, D_C)),
            _const_spec((H_C, CHUNK, CHUNK)),
            _const_spec((H_C, CHUNK, CHUNK)),
            _const_spec((D_C, d)),
            _const_spec((D_D, 2 * D_D)),
        ],
        out_specs=(pl.BlockSpec((None, tile, d), lambda b, t: (b, t, 0)),
                   pl.BlockSpec((None, 2, 2, tile, LANES), lambda b, t: (b, 0, 0, t, 0))),
        compiler_params=_params(2),
        name="mixer_cd",
    )(x, mod, ng, w_in, ln_g, ln_b, w_s, b_s_rows, w_out_c, cdft)


def _dft1_body(z_ref, w1_ref, a_ref, scr):
    n1 = z_ref.shape[2]
    for part in range(2):
        for half in range(2):
            scr[part, half] = z_ref[part, half].reshape(n1 * SUBLANES, LANES)
    cols = []
    for i in range(SUBLANES):
        for half in range(2):
            cols.append(jnp.concatenate(
                [scr[part, half, pl.ds(i, n1, stride=SUBLANES), :] for part in range(2)], axis=0))
    rhs = jnp.concatenate(cols, axis=1).astype(_BF16)
    a = _dot(w1_ref[...], rhs)
    for part in range(2):
        for half in range(2):
            for i in range(SUBLANES):
                lo = (2 * i + half) * LANES
                scr[part, half, pl.ds(i, n1, stride=SUBLANES), :] = a[part * n1:(part + 1) * n1, lo:lo + LANES]
    for part in range(2):
        for half in range(2):
            a_ref[part, half] = scr[part, half].reshape(n1, SUBLANES, LANES)


def _dft1_call(z, w1):
    bsz, _, _, seq, _ = z.shape
    n2 = DFT_N2
    n1 = seq // n2
    z6 = z.reshape(bsz, 2, 2, n1, n2, LANES)
    spec = pl.BlockSpec((None, 2, 2, n1, SUBLANES, LANES), lambda b, j: (b, 0, 0, 0, j, 0))
    return pl.pallas_call(
        _dft1_body,
        out_shape=jax.ShapeDtypeStruct(z6.shape, _F32),
        grid=(bsz, n2 // SUBLANES),
        in_specs=[spec, _const_spec((2 * n1, 2 * n1))],
        out_specs=spec,
        scratch_shapes=[pltpu.VMEM((2, 2, n1 * SUBLANES, LANES), _F32)],
        compiler_params=_params(2),
        name="seq_dft_stage1",
    )(z6, w1)


def _dft2_body(a_ref, m_ref, y_ref, scr):
    n2 = a_ref.shape[3]
    for i in range(SUBLANES):
        re = jnp.concatenate([a_ref[0, 0, i], a_ref[0, 1, i]], axis=1)
        im = jnp.concatenate([a_ref[1, 0, i], a_ref[1, 1, i]], axis=1)
        rhs = jnp.concatenate([re, im], axis=0).astype(_BF16)
        y = _dot(m_ref[i], rhs)
        for half in range(2):
            scr[half, pl.ds(i, n2, stride=SUBLANES), :] = y[:, half * LANES:(half + 1) * LANES]
    for half in range(2):
        y_ref[half] = scr[half].reshape(n2, SUBLANES, LANES)


def _dft2_call(a6, m2):
    bsz, _, _, n1, n2, _ = a6.shape
    seq = n1 * n2
    y = pl.pallas_call(
        _dft2_body,
        out_shape=jax.ShapeDtypeStruct((bsz, 2, n2, n1, LANES), _F32),
        grid=(bsz, n1 // SUBLANES),
        in_specs=[
            pl.BlockSpec((None, 2, 2, SUBLANES, n2, LANES), lambda b, j: (b, 0, 0, j, 0, 0)),
            pl.BlockSpec((SUBLANES, n2, 2 * n2), lambda b, j: (j, 0, 0)),
        ],
        out_specs=pl.BlockSpec((None, 2, n2, SUBLANES, LANES), lambda b, j: (b, 0, 0, j, 0)),
        scratch_shapes=[pltpu.VMEM((2, n2 * SUBLANES, LANES), _F32)],
        compiler_params=_params(2),
        name="seq_dft_stage2",
    )(a6, m2)
    return y.reshape(bsz, 2, seq, LANES)


def _channel_dft_matrix():
    c = np.arange(D_D)
    same = (c[:, None] // DG_D) == (c[None, :] // DG_D)
    ang = 2.0 * np.pi * ((c[:, None] % DG_D) * (c[None, :] % DG_D) % DG_D) / DG_D
    scale = 1.0 / np.sqrt(DG_D)
    re = np.where(same, np.cos(ang), 0.0) * scale
    im = np.where(same, -np.sin(ang), 0.0) * scale
    return np.concatenate([re, im], axis=1)


def _stage1_matrix(n1):
    k = np.arange(n1)
    ang = 2.0 * np.pi * ((k[:, None] * k[None, :]) % n1) / n1
    c, s = np.cos(ang), np.sin(ang)
    return np.block([[c, s], [-s, c]])


def _stage2_matrices(n1, n2):
    seq = n1 * n2
    k1 = np.arange(n1)[:, None, None]
    k2 = np.arange(n2)[None, :, None]
    nn = np.arange(n2)[None, None, :]
    ang = 2.0 * np.pi * ((nn * (k1 + n1 * k2)) % seq) / seq
    scale = 1.0 / np.sqrt(seq)
    return np.concatenate([np.cos(ang), np.sin(ang)], axis=2) * scale


def _trunk(x, b_off, mod, wts):
    seq = x.shape[1]
    n1 = seq // DFT_N2
    x = _layer0_call(x, mod[0], b_off, wts["mix_g"][0], wts["ab_w_in"], wts["ab_conv_a"], wts["ab_conv_b_w"],
                     wts["ab_conv_b_b"], wts["ab_ln_g"], wts["ab_ln_b"], wts["ab_w_out"],
                     wts["ffn_g"][0], wts["ffn_w_in"][0], wts["ffn_w_out"][0])
    x, z = _mix_cd_call(x, mod[1], b_off, wts["mix_g"][1], wts["cd_w_in"], wts["cd_ln_g"], wts["cd_ln_b"],
                        wts["cd_w_s"], wts["cd_b_s_rows"], wts["cd_w_out_c"], wts["cdft"])
    a = _dft1_call(z, _table(_stage1_matrix(n1)))
    yd = _dft2_call(a, _table(_stage2_matrices(n1, DFT_N2)))
    return _ffn_final_call(x, yd, mod[1], b_off, wts["ffn_g"][1], wts["ffn_w_in"][1], wts["ffn_w_out"][1],
                           wts["cd_w_out_d"], wts["final_g"])


def _table(values):
    return jnp.asarray(values, _F32).astype(_BF16)


def _mix_in_columns():
    half = MXU_DIM
    cols = []
    for first, second in ((3 * D_A, 3 * D_A + D_B), (D_A, 2 * D_A)):
        for lo in range(0, D_B, half):
            cols += list(range(first + lo, first + lo + half)) + list(range(second + lo, second + lo + half))
    return np.asarray(cols + list(range(0, D_A)), np.int32)


def _tap_rows(w):
    return jnp.broadcast_to(w[:, None, :], (w.shape[0], SUBLANES, w.shape[1]))


def kernel(x_prompt, x_sample, c_prompt, c_sample, ada_w, ada_b, mix_norm_g, ffn_norm_g, ab_w_in, ab_conv_a,
           ab_conv_b_w, ab_conv_b_b, ab_ln_g, ab_ln_b, ab_w_out, cd_w_in, cd_ln_g, cd_ln_b, cd_w_s, cd_b_s,
           cd_w_out, ffn_w_in, ffn_w_out, final_g):
    depth = ada_w.shape[0]
    assert depth == 2 and ab_w_in.shape[0] == 1 and cd_w_in.shape[0] == 1
    n_prompt, n_sample = c_prompt.shape[0], c_sample.shape[0]
    assert n_prompt + n_sample <= MOD_ROWS

    c_all = jnp.concatenate(
        [c_prompt, c_sample, jnp.zeros((MOD_ROWS - n_prompt - n_sample, D_MODEL), _F32)], axis=0)
    mod = _mod_call(c_all, ada_w, ada_b).reshape(depth, MOD_ROWS, 1, N_MOD * D_MODEL)

    wts = {
        "mix_g": mix_norm_g.reshape(depth, 1, D_MODEL),
        "ffn_g": ffn_norm_g.reshape(depth, 1, D_MODEL),
        "ab_w_in": ab_w_in[0][:, _mix_in_columns()].astype(_BF16),
        "ab_conv_a": _tap_rows(ab_conv_a[0]),
        "ab_conv_b_w": _tap_rows(ab_conv_b_w[0]),
        "ab_conv_b_b": ab_conv_b_b,
        "ab_ln_g": ab_ln_g,
        "ab_ln_b": ab_ln_b,
        "ab_w_out": ab_w_out[0].astype(_BF16),
        "cd_w_in": cd_w_in[0].astype(_BF16),
        "cd_ln_g": cd_ln_g,
        "cd_ln_b": cd_ln_b,
        "cd_w_s": cd_w_s[0].astype(_BF16),
        "cd_b_s_rows": jnp.broadcast_to(cd_b_s[0][:, :, None], (H_C, CHUNK, CHUNK)),
        "cd_w_out_c": cd_w_out[0, :D_C].astype(_BF16),
        "cd_w_out_d": cd_w_out[0, D_C:].astype(_BF16),
        "cdft": _table(_channel_dft_matrix()),
        "ffn_w_in": ffn_w_in.astype(_BF16),
        "ffn_w_out": ffn_w_out.astype(_BF16),
        "final_g": final_g.reshape(1, D_MODEL),
    }
    y_prompt = _trunk(x_prompt, 0, mod, wts)
    y_sample = _trunk(x_sample, n_prompt, mod, wts)
    return (y_prompt, y_sample)
```

```python
import jax
import jax.numpy as jnp
import numpy as np
from jax import lax
from jax.experimental import pallas as pl
from jax.experimental.pallas import tpu as pltpu

D_MODEL = 1024
D_A = 512
D_B = 512
K_A = 3
K_B = 31
D_C = 768
H_C = 6
CHUNK = 128
D_D = 256
DG_D = 64
D_FF = 2816
N_MOD = 6
EPS = 1e-6

LANES = 128
SUBLANES = 8
MXU_DIM = 256
TOKEN_TILE = 512
SUB_TILES = 2
HALO = 16
CONV_STRIDE = 4
CONV_GROUP = SUBLANES * CONV_STRIDE
CONV_ACCS = 8
FFN_SPLIT = 6 * MXU_DIM
DFT_N2 = 128
MOD_ROWS = 16
MOD_COLS = 1536
VMEM_LIMIT = 56 * 1024 * 1024

_BF16 = jnp.bfloat16
_F32 = jnp.float32


def _const_spec(shape):
    nd = len(shape)
    return pl.BlockSpec(shape, lambda *_: (0,) * nd, pipeline_mode=pl.Buffered(1))


def _params(n_grid):
    return pltpu.CompilerParams(
        dimension_semantics=("parallel",) * n_grid, vmem_limit_bytes=VMEM_LIMIT)


def _dot(a, b):
    return jnp.dot(a, b, preferred_element_type=_F32)


def _modulated_rmsnorm(x, gain, scale, shift):
    ms = jnp.mean(x * x, axis=-1, keepdims=True)
    return x * lax.rsqrt(ms + EPS) * (gain * (1.0 + scale)) + shift


def _layernorm(x, g, b):
    mu = jnp.mean(x, axis=-1, keepdims=True)
    xc = x - mu
    var = jnp.mean(xc * xc, axis=-1, keepdims=True)
    return xc * lax.rsqrt(var + EPS) * g + b


def _mod_slices(mod):
    d = D_MODEL
    return [mod[:, i * d:(i + 1) * d] for i in range(N_MOD)]


def _swiglu_residual(x, mod, ng_ref, win_ref, wout_ref):
    _, _, _, sh2, sc2, g2 = _mod_slices(mod)
    h = _modulated_rmsnorm(x, ng_ref[...], sc2, sh2).astype(_BF16)
    acc = None
    for lo, hi in ((0, FFN_SPLIT), (FFN_SPLIT, D_FF)):
        gate = _dot(h, win_ref[:, lo:hi])
        up = _dot(h, win_ref[:, D_FF + lo:D_FF + hi])
        a = (gate * jax.nn.sigmoid(gate) * up).astype(_BF16)
        part = _dot(a, wout_ref[lo:hi, :])
        acc = part if acc is None else acc + part
    return x + g2 * acc


def _mod_body(c_ref, w_ref, b_ref, o_ref):
    c = c_ref[...]
    sc = (c * jax.nn.sigmoid(c)).astype(_BF16)
    o_ref[...] = _dot(sc, w_ref[...].astype(_BF16)) + b_ref[...]


def _mod_call(c_all, ada_w, ada_b):
    depth = ada_w.shape[0]
    n_out = ada_w.shape[2]
    return pl.pallas_call(
        _mod_body,
        out_shape=jax.ShapeDtypeStruct((depth, MOD_ROWS, n_out), _F32),
        grid=(depth, n_out // MOD_COLS),
        in_specs=[
            pl.BlockSpec((MOD_ROWS, D_MODEL), lambda l, n: (0, 0)),
            pl.BlockSpec((None, D_MODEL, MOD_COLS), lambda l, n: (l, 0, n)),
            pl.BlockSpec((None, 1, MOD_COLS), lambda l, n: (l, 0, n)),
        ],
        out_specs=pl.BlockSpec((None, MOD_ROWS, MOD_COLS), lambda l, n: (l, 0, n)),
        compiler_params=_params(2),
        name="adaln_mod",
    )(c_all, ada_w, ada_b.reshape(depth, 1, n_out))


def _dwconv(src_ref, w_ref, first_row, n_tokens, dst_ref, slabs):
    n_taps = w_ref.shape[0]
    n_vregs = n_tokens // SUBLANES
    for c in slabs:
        lanes = slice(c * LANES, (c + 1) * LANES)
        for v0 in range(0, n_vregs, CONV_ACCS):
            starts = [CONV_GROUP * (v // CONV_STRIDE) + v % CONV_STRIDE for v in range(v0, v0 + CONV_ACCS)]
            accs = [None] * CONV_ACCS
            for k in range(n_taps):
                w = w_ref[k, :, lanes]
                for q, t0 in enumerate(starts):
                    term = w * src_ref[c, pl.ds(first_row + k + t0, SUBLANES, stride=CONV_STRIDE), :]
                    accs[q] = term if accs[q] is None else accs[q] + term
            for q, t0 in enumerate(starts):
                dst_ref[c, pl.ds(t0, SUBLANES, stride=CONV_STRIDE), :] = accs[q]


def _after(x, dep, zeros_ref):
    bits = pltpu.bitcast(dep, jnp.int32)
    acc = None
    for r in range(0, dep.shape[0], SUBLANES):
        for c in range(0, dep.shape[1], LANES):
            blk = bits[r:r + SUBLANES, c:c + LANES]
            acc = blk if acc is None else acc | blk
    z = pltpu.bitcast(acc & zeros_ref[...], _F32)
    row = jnp.concatenate([z] * (x.shape[1] // LANES), axis=1)
    return x + jnp.concatenate([row] * (x.shape[0] // SUBLANES), axis=0)


def _layer0_body(xp_ref, x_ref, xn_ref, mod_ref, ng_ref, win_ref, ca_ref, cbw_ref, cbb_ref,
                 lng_ref, lnb_ref, wout_ref, fg_ref, fwin_ref, fwout_ref, zeros_ref, o_ref,
                 u_scr, g_scr, ca_scr, cb_scr):
    t = pl.program_id(1)
    nt = pl.num_programs(1)
    tile = x_ref.shape[0]
    sub = tile // SUB_TILES
    half = MXU_DIM
    slabs_per_half = half // LANES
    mod = mod_ref[...]
    sh1, sc1, g1, _, _, _ = _mod_slices(mod)

    xin = jnp.concatenate([xp_ref[...], x_ref[...], xn_ref[...]], axis=0)
    h = _modulated_rmsnorm(xin, ng_ref[...], sc1, sh1).astype(_BF16)
    all_rows = tile + 2 * HALO
    r = lax.broadcasted_iota(jnp.int32, (all_rows, 1), 0)
    valid = jnp.logical_and(jnp.logical_or(r >= HALO, t > 0),
                            jnp.logical_or(r < HALO + tile, t < nt - 1))

    for j in range(D_B // half):
        q = _dot(h, win_ref[:, 2 * half * j:2 * half * (j + 1)])
        g = jnp.where(valid, q[:, :half] * jax.nn.sigmoid(q[:, half:]), 0.0)
        slabs = range(slabs_per_half * j, slabs_per_half * (j + 1))
        for c in slabs:
            lo = (c - slabs[0]) * LANES
            g_scr[c] = g[:, lo:lo + LANES]
        _dwconv(g_scr, cbw_ref, HALO - K_B // 2, tile, cb_scr, slabs)
    for j in range(D_A // half):
        base = 2 * D_B + 2 * half * j
        q = _dot(h, win_ref[:, base:base + 2 * half])
        u = jnp.where(valid, q[:, :half] * q[:, half:], 0.0)
        slabs = range(slabs_per_half * j, slabs_per_half * (j + 1))
        for c in slabs:
            lo = (c - slabs[0]) * LANES
            u_scr[c] = u[:, lo:lo + LANES]
        _dwconv(u_scr, ca_ref, HALO - K_A // 2, tile, ca_scr, slabs)
    a_b_all = _dot(h[HALO:HALO + tile, :], win_ref[:, 2 * D_B + 2 * D_A:])
    a_bs = [a_b_all[s * sub:(s + 1) * sub, :] for s in range(SUB_TILES)]

    prev_out = None
    for s in range(SUB_TILES):
        tok = slice(s * sub, (s + 1) * sub)
        y_a = a_bs[s] * jnp.concatenate([ca_scr[c, tok, :] for c in range(D_A // LANES)], axis=1)
        z = _layernorm(jnp.concatenate([cb_scr[c, tok, :] for c in range(D_B // LANES)], axis=1) + cbb_ref[...],
                       lng_ref[...], lnb_ref[...])
        y_b = z * jax.nn.sigmoid(z)
        gate_dep = a_bs[-1][sub - SUBLANES:, :]
        y_a = _after(y_a, gate_dep, zeros_ref)
        y_b = _after(y_b, gate_dep, zeros_ref)

        m = _dot(y_a.astype(_BF16), wout_ref[0:D_A, :]) + _dot(y_b.astype(_BF16), wout_ref[D_A:, :])
        x1 = x_ref[s * sub:(s + 1) * sub, :] + g1 * m
        if prev_out is not None:
            x1 = _after(x1, prev_out[sub - SUBLANES:, :], zeros_ref)
        prev_out = _swiglu_residual(x1, mod, fg_ref, fwin_ref, fwout_ref)
        o_ref[s * sub:(s + 1) * sub, :] = prev_out


def _layer0_call(x, mod, b_off, ng, w_in, conv_a, conv_b_w, conv_b_b, ln_g, ln_b, w_out, fg, f_w_in, f_w_out):
    bsz, seq, d = x.shape
    tile = TOKEN_TILE
    per = tile // HALO
    n_halo_blocks = seq // HALO
    n_in = w_in.shape[1]
    return pl.pallas_call(
        _layer0_body,
        out_shape=jax.ShapeDtypeStruct(x.shape, _F32),
        grid=(bsz, seq // tile),
        in_specs=[
            pl.BlockSpec((None, HALO, d), lambda b, t: (b, jnp.maximum(t * per - 1, 0), 0)),
            pl.BlockSpec((None, tile, d), lambda b, t: (b, t, 0)),
            pl.BlockSpec((None, HALO, d), lambda b, t: (b, jnp.minimum((t + 1) * per, n_halo_blocks - 1), 0)),
            pl.BlockSpec((None, 1, N_MOD * d), lambda b, t: (b + b_off, 0, 0)),
            _const_spec((1, d)),
            _const_spec((d, n_in)),
            _const_spec((K_A, SUBLANES, D_A)),
            _const_spec((K_B, SUBLANES, D_B)),
            _const_spec((1, D_B)),
            _const_spec((1, D_B)),
            _const_spec((1, D_B)),
            _const_spec((D_A + D_B, d)),
            _const_spec((1, d)),
            _const_spec((d, 2 * D_FF)),
            _const_spec((D_FF, d)),
            _const_spec((SUBLANES, LANES)),
        ],
        out_specs=pl.BlockSpec((None, tile, d), lambda b, t: (b, t, 0)),
        scratch_shapes=[pltpu.VMEM((D_A // LANES, tile + 2 * HALO, LANES), _F32),
                        pltpu.VMEM((D_B // LANES, tile + 2 * HALO, LANES), _F32),
                        pltpu.VMEM((D_A // LANES, tile, LANES), _F32),
                        pltpu.VMEM((D_B // LANES, tile, LANES), _F32)],
        compiler_params=_params(2),
        name="layer0",
    )(x, x, x, mod, ng, w_in, conv_a, conv_b_w, conv_b_b, ln_g, ln_b, w_out, fg, f_w_in, f_w_out,
      jnp.zeros((SUBLANES, LANES), jnp.int32))


def _ffn_final_body(x_ref, yd_ref, mod_ref, ng_ref, win_ref, wout_ref, wd_ref, fg_ref, o_ref):
    mod = mod_ref[...]
    g1 = mod[:, 2 * D_MODEL:3 * D_MODEL]
    yd = jnp.concatenate([yd_ref[0], yd_ref[1]], axis=1).astype(_BF16)
    x = x_ref[...] + g1 * _dot(yd, wd_ref[...])
    x = _swiglu_residual(x, mod, ng_ref, win_ref, wout_ref)
    ms = jnp.mean(x * x, axis=-1, keepdims=True)
    o_ref[...] = x * lax.rsqrt(ms + EPS) * fg_ref[...]


def _ffn_final_call(x, yd, mod, b_off, ng, w_in, w_out, w_d, final_g):
    bsz, seq, d = x.shape
    tile = TOKEN_TILE
    return pl.pallas_call(
        _ffn_final_body,
        out_shape=jax.ShapeDtypeStruct(x.shape, _F32),
        grid=(bsz, seq // tile),
        in_specs=[
            pl.BlockSpec((None, tile, d), lambda b, t: (b, t, 0)),
            pl.BlockSpec((None, 2, tile, LANES), lambda b, t: (b, 0, t, 0)),
            pl.BlockSpec((None, 1, N_MOD * d), lambda b, t: (b + b_off, 0, 0)),
            _const_spec((1, d)),
            _const_spec((d, 2 * D_FF)),
            _const_spec((D_FF, d)),
            _const_spec((D_D, d)),
            _const_spec((1, d)),
        ],
        out_specs=pl.BlockSpec((None, tile, d), lambda b, t: (b, t, 0)),
        compiler_params=_params(2),
        name="ffn_final",
    )(x, yd, mod, ng, w_in, w_out, w_d, final_g)


def _mix_cd_body(x_ref, mod_ref, ng_ref, win_ref, lng_ref, lnb_ref, ws_ref, bs_ref, woutc_ref,
                 cdft_ref, o_ref, z_ref):
    tile = x_ref.shape[0]
    n_chunks = tile // CHUNK
    x = x_ref[...]
    sh1, sc1, g1, _, _, _ = _mod_slices(mod_ref[...])
    h = _modulated_rmsnorm(x, ng_ref[...], sc1, sh1).astype(_BF16)
    p = _dot(h, win_ref[...])
    u = p[:, 0:D_C]
    v = _layernorm(p[:, D_C:2 * D_C], lng_ref[...], lnb_ref[...]).astype(_BF16)
    f = p[:, 2 * D_C:].astype(_BF16)

    head_cols = []
    for hd in range(H_C):
        lo = hd * CHUNK
        rhs = jnp.concatenate([v[n * CHUNK:(n + 1) * CHUNK, lo:lo + CHUNK] for n in range(n_chunks)], axis=1)
        sv = _dot(ws_ref[hd], rhs)
        bias = bs_ref[hd]
        head_cols.append(jnp.concatenate(
            [sv[:, n * CHUNK:(n + 1) * CHUNK] + bias for n in range(n_chunks)], axis=0))
    y_c = (u * jnp.concatenate(head_cols, axis=1)).astype(_BF16)
    o_ref[...] = x + g1 * _dot(y_c, woutc_ref[...])

    z = _dot(f, cdft_ref[...])
    for part in range(2):
        for half in range(2):
            lo = (2 * part + half) * LANES
            z_ref[part, half] = z[:, lo:lo + LANES]


def _mix_cd_call(x, mod, b_off, ng, w_in, ln_g, ln_b, w_s, b_s_rows, w_out_c, cdft):
    bsz, seq, d = x.shape
    tile = TOKEN_TILE
    n_in = w_in.shape[1]
    return pl.pallas_call(
        _mix_cd_body,
        out_shape=(jax.ShapeDtypeStruct(x.shape, _F32),
                   jax.ShapeDtypeStruct((bsz, 2, 2, seq, LANES), _F32)),
        grid=(bsz, seq // tile),
        in_specs=[
            pl.BlockSpec((None, tile, d), lambda b, t: (b, t, 0)),
            pl.BlockSpec((None, 1, N_MOD * d), lambda b, t: (b + b_off, 0, 0)),
            _const_spec((1, d)),
            _const_spec((d, n_in)),
            _const_spec((1, D_C)),
            _const_spec((1, D_C)),
            _const_spec((H_C, CHUNK, CHUNK)),
            _const_spec((H_C, CHUNK, CHUNK)),
            _const_spec((D_C, d)),
            _const_spec((D_D, 2 * D_D)),
        ],
        out_specs=(pl.BlockSpec((None, tile, d), lambda b, t: (b, t, 0)),
                   pl.BlockSpec((None, 2, 2, tile, LANES), lambda b, t: (b, 0, 0, t, 0))),
        compiler_params=_params(2),
        name="mixer_cd",
    )(x, mod, ng, w_in, ln_g, ln_b, w_s, b_s_rows, w_out_c, cdft)


def _dft1_body(z_ref, w1_ref, a_ref, scr):
    n1 = z_ref.shape[2]
    for part in range(2):
        for half in range(2):
            scr[part, half] = z_ref[part, half].reshape(n1 * SUBLANES, LANES)
    cols = []
    for i in range(SUBLANES):
        for half in range(2):
            cols.append(jnp.concatenate(
                [scr[part, half, pl.ds(i, n1, stride=SUBLANES), :] for part in range(2)], axis=0))
    rhs = jnp.concatenate(cols, axis=1).astype(_BF16)
    a = _dot(w1_ref[...], rhs)
    for part in range(2):
        for half in range(2):
            for i in range(SUBLANES):
                lo = (2 * i + half) * LANES
                scr[part, half, pl.ds(i, n1, stride=SUBLANES), :] = a[part * n1:(part + 1) * n1, lo:lo + LANES]
    for part in range(2):
        for half in range(2):
            a_ref[part, half] = scr[part, half].reshape(n1, SUBLANES, LANES)


def _dft1_call(z, w1):
    bsz, _, _, seq, _ = z.shape
    n2 = DFT_N2
    n1 = seq // n2
    z6 = z.reshape(bsz, 2, 2, n1, n2, LANES)
    spec = pl.BlockSpec((None, 2, 2, n1, SUBLANES, LANES), lambda b, j: (b, 0, 0, 0, j, 0))
    return pl.pallas_call(
        _dft1_body,
        out_shape=jax.ShapeDtypeStruct(z6.shape, _F32),
        grid=(bsz, n2 // SUBLANES),
        in_specs=[spec, _const_spec((2 * n1, 2 * n1))],
        out_specs=spec,
        scratch_shapes=[pltpu.VMEM((2, 2, n1 * SUBLANES, LANES), _F32)],
        compiler_params=_params(2),
        name="seq_dft_stage1",
    )(z6, w1)


def _dft2_body(a_ref, m_ref, y_ref, scr):
    n2 = a_ref.shape[3]
    for i in range(SUBLANES):
        re = jnp.concatenate([a_ref[0, 0, i], a_ref[0, 1, i]], axis=1)
        im = jnp.concatenate([a_ref[1, 0, i], a_ref[1, 1, i]], axis=1)
        rhs = jnp.concatenate([re, im], axis=0).astype(_BF16)
        y = _dot(m_ref[i], rhs)
        for half in range(2):
            scr[half, pl.ds(i, n2, stride=SUBLANES), :] = y[:, half * LANES:(half + 1) * LANES]
    for half in range(2):
        y_ref[half] = scr[half].reshape(n2, SUBLANES, LANES)


def _dft2_call(a6, m2):
    bsz, _, _, n1, n2, _ = a6.shape
    seq = n1 * n2
    y = pl.pallas_call(
        _dft2_body,
        out_shape=jax.ShapeDtypeStruct((bsz, 2, n2, n1, LANES), _F32),
        grid=(bsz, n1 // SUBLANES),
        in_specs=[
            pl.BlockSpec((None, 2, 2, SUBLANES, n2, LANES), lambda b, j: (b, 0, 0, j, 0, 0)),
            pl.BlockSpec((SUBLANES, n2, 2 * n2), lambda b, j: (j, 0, 0)),
        ],
        out_specs=pl.BlockSpec((None, 2, n2, SUBLANES, LANES), lambda b, j: (b, 0, 0, j, 0)),
        scratch_shapes=[pltpu.VMEM((2, n2 * SUBLANES, LANES), _F32)],
        compiler_params=_params(2),
        name="seq_dft_stage2",
    )(a6, m2)
    return y.reshape(bsz, 2, seq, LANES)


def _channel_dft_matrix():
    c = np.arange(D_D)
    same = (c[:, None] // DG_D) == (c[None, :] // DG_D)
    ang = 2.0 * np.pi * ((c[:, None] % DG_D) * (c[None, :] % DG_D) % DG_D) / DG_D
    scale = 1.0 / np.sqrt(DG_D)
    re = np.where(same, np.cos(ang), 0.0) * scale
    im = np.where(same, -np.sin(ang), 0.0) * scale
    return np.concatenate([re, im], axis=1)


def _stage1_matrix(n1):
    k = np.arange(n1)
    ang = 2.0 * np.pi * ((k[:, None] * k[None, :]) % n1) / n1
    c, s = np.cos(ang), np.sin(ang)
    return np.block([[c, s], [-s, c]])


def _stage2_matrices(n1, n2):
    seq = n1 * n2
    k1 = np.arange(n1)[:, None, None]
    k2 = np.arange(n2)[None, :, None]
    nn = np.arange(n2)[None, None, :]
    ang = 2.0 * np.pi * ((nn * (k1 + n1 * k2)) % seq) / seq
    scale = 1.0 / np.sqrt(seq)
    return np.concatenate([np.cos(ang), np.sin(ang)], axis=2) * scale


def _trunk(x, b_off, mod, wts):
    seq = x.shape[1]
    n1 = seq // DFT_N2
    x = _layer0_call(x, mod[0], b_off, wts["mix_g"][0], wts["ab_w_in"], wts["ab_conv_a"], wts["ab_conv_b_w"],
                     wts["ab_conv_b_b"], wts["ab_ln_g"], wts["ab_ln_b"], wts["ab_w_out"],
                     wts["ffn_g"][0], wts["ffn_w_in"][0], wts["ffn_w_out"][0])
    x, z = _mix_cd_call(x, mod[1], b_off, wts["mix_g"][1], wts["cd_w_in"], wts["cd_ln_g"], wts["cd_ln_b"],
                        wts["cd_w_s"], wts["cd_b_s_rows"], wts["cd_w_out_c"], wts["cdft"])
    a = _dft1_call(z, _table(_stage1_matrix(n1)))
    yd = _dft2_call(a, _table(_stage2_matrices(n1, DFT_N2)))
    return _ffn_final_call(x, yd, mod[1], b_off, wts["ffn_g"][1], wts["ffn_w_in"][1], wts["ffn_w_out"][1],
                           wts["cd_w_out_d"], wts["final_g"])


def _table(values):
    return jnp.asarray(values, _F32).astype(_BF16)


def _mix_in_columns():
    half = MXU_DIM
    cols = []
    for first, second in ((3 * D_A, 3 * D_A + D_B), (D_A, 2 * D_A)):
        for lo in range(0, D_B, half):
            cols += list(range(first + lo, first + lo + half)) + list(range(second + lo, second + lo + half))
    return np.asarray(cols + list(range(0, D_A)), np.int32)


def _tap_rows(w):
    return jnp.broadcast_to(w[:, None, :], (w.shape[0], SUBLANES, w.shape[1]))


def kernel(x_prompt, x_sample, c_prompt, c_sample, ada_w, ada_b, mix_norm_g, ffn_norm_g, ab_w_in, ab_conv_a,
           ab_conv_b_w, ab_conv_b_b, ab_ln_g, ab_ln_b, ab_w_out, cd_w_in, cd_ln_g, cd_ln_b, cd_w_s, cd_b_s,
           cd_w_out, ffn_w_in, ffn_w_out, final_g):
    depth = ada_w.shape[0]
    assert depth == 2 and ab_w_in.shape[0] == 1 and cd_w_in.shape[0] == 1
    n_prompt, n_sample = c_prompt.shape[0], c_sample.shape[0]
    assert n_prompt + n_sample <= MOD_ROWS

    c_all = jnp.concatenate(
        [c_prompt, c_sample, jnp.zeros((MOD_ROWS - n_prompt - n_sample, D_MODEL), _F32)], axis=0)
    mod = _mod_call(c_all, ada_w, ada_b).reshape(depth, MOD_ROWS, 1, N_MOD * D_MODEL)

    wts = {
        "mix_g": mix_norm_g.reshape(depth, 1, D_MODEL),
        "ffn_g": ffn_norm_g.reshape(depth, 1, D_MODEL),
        "ab_w_in": ab_w_in[0][:, _mix_in_columns()].astype(_BF16),
        "ab_conv_a": _tap_rows(ab_conv_a[0]),
        "ab_conv_b_w": _tap_rows(ab_conv_b_w[0]),
        "ab_conv_b_b": ab_conv_b_b,
        "ab_ln_g": ab_ln_g,
        "ab_ln_b": ab_ln_b,
        "ab_w_out": ab_w_out[0].astype(_BF16),
        "cd_w_in": cd_w_in[0].astype(_BF16),
        "cd_ln_g": cd_ln_g,
        "cd_ln_b": cd_ln_b,
        "cd_w_s": cd_w_s[0].astype(_BF16),
        "cd_b_s_rows": jnp.broadcast_to(cd_b_s[0][:, :, None], (H_C, CHUNK, CHUNK)),
        "cd_w_out_c": cd_w_out[0, :D_C].astype(_BF16),
        "cd_w_out_d": cd_w_out[0, D_C:].astype(_BF16),
        "cdft": _table(_channel_dft_matrix()),
        "ffn_w_in": ffn_w_in.astype(_BF16),
        "ffn_w_out": ffn_w_out.astype(_BF16),
        "final_g": final_g.reshape(1, D_MODEL),
    }
    y_prompt = _trunk(x_prompt, 0, mod, wts)
    y_sample = _trunk(x_sample, n_prompt, mod, wts)
    return (y_prompt, y_sample)
```

```python
import jax
import jax.numpy as jnp
import numpy as np
from jax import lax
from jax.experimental import pallas as pl
from jax.experimental.pallas import tpu as pltpu

D_MODEL = 1024
D_A = 512
D_B = 512
K_A = 3
K_B = 31
D_C = 768
H_C = 6
CHUNK = 128
D_D = 256
DG_D = 64
D_FF = 2816
N_MOD = 6
EPS = 1e-6

LANES = 128
SUBLANES = 8
MXU_DIM = 256
TOKEN_TILE = 512
CD_TOKEN_TILE = 1024
SUB_TILES = 2
HALO = 16
CONV_STRIDE = 4
CONV_GROUP = SUBLANES * CONV_STRIDE
CONV_ACCS = 8
FFN_SPLIT = 6 * MXU_DIM
DFT_N2 = 128
MOD_ROWS = 16
MOD_COLS = 1536
VMEM_LIMIT = 56 * 1024 * 1024

_BF16 = jnp.bfloat16
_F32 = jnp.float32


def _const_spec(shape):
    nd = len(shape)
    return pl.BlockSpec(shape, lambda *_: (0,) * nd, pipeline_mode=pl.Buffered(1))


def _params(n_grid):
    return pltpu.CompilerParams(
        dimension_semantics=("parallel",) * n_grid, vmem_limit_bytes=VMEM_LIMIT)


def _dot(a, b):
    return jnp.dot(a, b, preferred_element_type=_F32)


def _modulated_rmsnorm(x, gain, scale, shift):
    ms = jnp.mean(x * x, axis=-1, keepdims=True)
    return x * lax.rsqrt(ms + EPS) * (gain * (1.0 + scale)) + shift


def _layernorm(x, g, b):
    mu = jnp.mean(x, axis=-1, keepdims=True)
    xc = x - mu
    var = jnp.mean(xc * xc, axis=-1, keepdims=True)
    return xc * lax.rsqrt(var + EPS) * g + b


def _mod_slices(mod):
    d = D_MODEL
    return [mod[:, i * d:(i + 1) * d] for i in range(N_MOD)]


def _swiglu_residual(x, mod, ng_ref, win_ref, wout_ref):
    _, _, _, sh2, sc2, g2 = _mod_slices(mod)
    h = _modulated_rmsnorm(x, ng_ref[...], sc2, sh2).astype(_BF16)
    acc = None
    for lo, hi in ((0, FFN_SPLIT), (FFN_SPLIT, D_FF)):
        gate = _dot(h, win_ref[:, lo:hi])
        up = _dot(h, win_ref[:, D_FF + lo:D_FF + hi])
        a = (gate * jax.nn.sigmoid(gate) * up).astype(_BF16)
        part = _dot(a, wout_ref[lo:hi, :])
        acc = part if acc is None else acc + part
    return x + g2 * acc


def _mod_body(c_ref, w_ref, b_ref, o_ref):
    c = c_ref[...]
    sc = (c * jax.nn.sigmoid(c)).astype(_BF16)
    o_ref[...] = _dot(sc, w_ref[...].astype(_BF16)) + b_ref[...]


def _mod_call(c_all, ada_w, ada_b):
    depth = ada_w.shape[0]
    n_out = ada_w.shape[2]
    return pl.pallas_call(
        _mod_body,
        out_shape=jax.ShapeDtypeStruct((depth, MOD_ROWS, n_out), _F32),
        grid=(depth, n_out // MOD_COLS),
        in_specs=[
            pl.BlockSpec((MOD_ROWS, D_MODEL), lambda l, n: (0, 0)),
            pl.BlockSpec((None, D_MODEL, MOD_COLS), lambda l, n: (l, 0, n)),
            pl.BlockSpec((None, 1, MOD_COLS), lambda l, n: (l, 0, n)),
        ],
        out_specs=pl.BlockSpec((None, MOD_ROWS, MOD_COLS), lambda l, n: (l, 0, n)),
        compiler_params=_params(2),
        name="adaln_mod",
    )(c_all, ada_w, ada_b.reshape(depth, 1, n_out))


def _dwconv(src_ref, w_ref, first_row, n_tokens, dst_ref, slabs):
    n_taps = w_ref.shape[0]
    n_vregs = n_tokens // SUBLANES
    for c in slabs:
        lanes = slice(c * LANES, (c + 1) * LANES)
        for v0 in range(0, n_vregs, CONV_ACCS):
            starts = [CONV_GROUP * (v // CONV_STRIDE) + v % CONV_STRIDE for v in range(v0, v0 + CONV_ACCS)]
            accs = [None] * CONV_ACCS
            for k in range(n_taps):
                w = w_ref[k, :, lanes]
                for q, t0 in enumerate(starts):
                    term = w * src_ref[c, pl.ds(first_row + k + t0, SUBLANES, stride=CONV_STRIDE), :]
                    accs[q] = term if accs[q] is None else accs[q] + term
            for q, t0 in enumerate(starts):
                dst_ref[c, pl.ds(t0, SUBLANES, stride=CONV_STRIDE), :] = accs[q]


def _after(x, dep, zeros_ref):
    bits = pltpu.bitcast(dep, jnp.int32)
    acc = None
    for r in range(0, dep.shape[0], SUBLANES):
        for c in range(0, dep.shape[1], LANES):
            blk = bits[r:r + SUBLANES, c:c + LANES]
            acc = blk if acc is None else acc | blk
    z = pltpu.bitcast(acc & zeros_ref[...], _F32)
    row = jnp.concatenate([z] * (x.shape[1] // LANES), axis=1)
    return x + jnp.concatenate([row] * (x.shape[0] // SUBLANES), axis=0)


def _layer0_body(xp_ref, x_ref, xn_ref, mod_ref, ng_ref, win_ref, ca_ref, cbw_ref, cbb_ref,
                 lng_ref, lnb_ref, wout_ref, fg_ref, fwin_ref, fwout_ref, zeros_ref, o_ref,
                 u_scr, g_scr, ca_scr, cb_scr):
    t = pl.program_id(1)
    nt = pl.num_programs(1)
    tile = x_ref.shape[0]
    sub = tile // SUB_TILES
    half = MXU_DIM
    slabs_per_half = half // LANES
    mod = mod_ref[...]
    sh1, sc1, g1, _, _, _ = _mod_slices(mod)

    xin = jnp.concatenate([xp_ref[...], x_ref[...], xn_ref[...]], axis=0)
    h = _modulated_rmsnorm(xin, ng_ref[...], sc1, sh1).astype(_BF16)
    all_rows = tile + 2 * HALO
    r = lax.broadcasted_iota(jnp.int32, (all_rows, 1), 0)
    valid = jnp.logical_and(jnp.logical_or(r >= HALO, t > 0),
                            jnp.logical_or(r < HALO + tile, t < nt - 1))

    for j in range(D_B // half):
        q = _dot(h, win_ref[:, 2 * half * j:2 * half * (j + 1)])
        g = jnp.where(valid, q[:, :half] * jax.nn.sigmoid(q[:, half:]), 0.0)
        slabs = range(slabs_per_half * j, slabs_per_half * (j + 1))
        for c in slabs:
            lo = (c - slabs[0]) * LANES
            g_scr[c] = g[:, lo:lo + LANES]
        _dwconv(g_scr, cbw_ref, HALO - K_B // 2, tile, cb_scr, slabs)
    for j in range(D_A // half):
        base = 2 * D_B + 2 * half * j
        q = _dot(h, win_ref[:, base:base + 2 * half])
        u = jnp.where(valid, q[:, :half] * q[:, half:], 0.0)
        slabs = range(slabs_per_half * j, slabs_per_half * (j + 1))
        for c in slabs:
            lo = (c - slabs[0]) * LANES
            u_scr[c] = u[:, lo:lo + LANES]
        _dwconv(u_scr, ca_ref, HALO - K_A // 2, tile, ca_scr, slabs)
    a_b_all = _dot(h[HALO:HALO + tile, :], win_ref[:, 2 * D_B + 2 * D_A:])
    a_bs = [a_b_all[s * sub:(s + 1) * sub, :] for s in range(SUB_TILES)]

    prev_out = None
    for s in range(SUB_TILES):
        tok = slice(s * sub, (s + 1) * sub)
        y_a = a_bs[s] * jnp.concatenate([ca_scr[c, tok, :] for c in range(D_A // LANES)], axis=1)
        z = _layernorm(jnp.concatenate([cb_scr[c, tok, :] for c in range(D_B // LANES)], axis=1) + cbb_ref[...],
                       lng_ref[...], lnb_ref[...])
        y_b = z * jax.nn.sigmoid(z)
        gate_dep = a_bs[-1][sub - SUBLANES:, :]
        y_a = _after(y_a, gate_dep, zeros_ref)
        y_b = _after(y_b, gate_dep, zeros_ref)

        m = _dot(y_a.astype(_BF16), wout_ref[0:D_A, :]) + _dot(y_b.astype(_BF16), wout_ref[D_A:, :])
        x1 = x_ref[s * sub:(s + 1) * sub, :] + g1 * m
        if prev_out is not None:
            x1 = _after(x1, prev_out[sub - SUBLANES:, :], zeros_ref)
        prev_out = _swiglu_residual(x1, mod, fg_ref, fwin_ref, fwout_ref)
        o_ref[s * sub:(s + 1) * sub, :] = prev_out


def _layer0_call(x, mod, b_off, ng, w_in, conv_a, conv_b_w, conv_b_b, ln_g, ln_b, w_out, fg, f_w_in, f_w_out):
    bsz, seq, d = x.shape
    tile = TOKEN_TILE
    per = tile // HALO
    n_halo_blocks = seq // HALO
    n_in = w_in.shape[1]
    return pl.pallas_call(
        _layer0_body,
        out_shape=jax.ShapeDtypeStruct(x.shape, _F32),
        grid=(bsz, seq // tile),
        in_specs=[
            pl.BlockSpec((None, HALO, d), lambda b, t: (b, jnp.maximum(t * per - 1, 0), 0)),
            pl.BlockSpec((None, tile, d), lambda b, t: (b, t, 0)),
            pl.BlockSpec((None, HALO, d), lambda b, t: (b, jnp.minimum((t + 1) * per, n_halo_blocks - 1), 0)),
            pl.BlockSpec((None, 1, N_MOD * d), lambda b, t: (b + b_off, 0, 0)),
            _const_spec((1, d)),
            _const_spec((d, n_in)),
            _const_spec((K_A, SUBLANES, D_A)),
            _const_spec((K_B, SUBLANES, D_B)),
            _const_spec((1, D_B)),
            _const_spec((1, D_B)),
            _const_spec((1, D_B)),
            _const_spec((D_A + D_B, d)),
            _const_spec((1, d)),
            _const_spec((d, 2 * D_FF)),
            _const_spec((D_FF, d)),
            _const_spec((SUBLANES, LANES)),
        ],
        out_specs=pl.BlockSpec((None, tile, d), lambda b, t: (b, t, 0)),
        scratch_shapes=[pltpu.VMEM((D_A // LANES, tile + 2 * HALO, LANES), _F32),
                        pltpu.VMEM((D_B // LANES, tile + 2 * HALO, LANES), _F32),
                        pltpu.VMEM((D_A // LANES, tile, LANES), _F32),
                        pltpu.VMEM((D_B // LANES, tile, LANES), _F32)],
        compiler_params=_params(2),
        name="layer0",
    )(x, x, x, mod, ng, w_in, conv_a, conv_b_w, conv_b_b, ln_g, ln_b, w_out, fg, f_w_in, f_w_out,
      jnp.zeros((SUBLANES, LANES), jnp.int32))


def _ffn_final_body(x_ref, yd_ref, mod_ref, ng_ref, win_ref, wout_ref, wd_ref, fg_ref, o_ref):
    mod = mod_ref[...]
    g1 = mod[:, 2 * D_MODEL:3 * D_MODEL]
    yd = jnp.concatenate([yd_ref[0], yd_ref[1]], axis=1).astype(_BF16)
    x = x_ref[...] + g1 * _dot(yd, wd_ref[...])
    x = _swiglu_residual(x, mod, ng_ref, win_ref, wout_ref)
    ms = jnp.mean(x * x, axis=-1, keepdims=True)
    o_ref[...] = x * lax.rsqrt(ms + EPS) * fg_ref[...]


def _ffn_final_call(x, yd, mod, b_off, ng, w_in, w_out, w_d, final_g):
    bsz, seq, d = x.shape
    tile = TOKEN_TILE
    return pl.pallas_call(
        _ffn_final_body,
        out_shape=jax.ShapeDtypeStruct(x.shape, _F32),
        grid=(bsz, seq // tile),
        in_specs=[
            pl.BlockSpec((None, tile, d), lambda b, t: (b, t, 0)),
            pl.BlockSpec((None, 2, tile, LANES), lambda b, t: (b, 0, t, 0)),
            pl.BlockSpec((None, 1, N_MOD * d), lambda b, t: (b + b_off, 0, 0)),
            _const_spec((1, d)),
            _const_spec((d, 2 * D_FF)),
            _const_spec((D_FF, d)),
            _const_spec((D_D, d)),
            _const_spec((1, d)),
        ],
        out_specs=pl.BlockSpec((None, tile, d), lambda b, t: (b, t, 0)),
        compiler_params=_params(2),
        name="ffn_final",
    )(x, yd, mod, ng, w_in, w_out, w_d, final_g)


def _mix_cd_body(x_ref, mod_ref, ng_ref, win_ref, lng_ref, lnb_ref, ws_ref, bs_ref, woutc_ref,
                 cdft_ref, o_ref, z_ref):
    tile = x_ref.shape[0]
    n_chunks = tile // CHUNK
    x = x_ref[...]
    sh1, sc1, g1, _, _, _ = _mod_slices(mod_ref[...])
    h = _modulated_rmsnorm(x, ng_ref[...], sc1, sh1).astype(_BF16)
    p = _dot(h, win_ref[...])
    u = p[:, 0:D_C]
    v = _layernorm(p[:, D_C:2 * D_C], lng_ref[...], lnb_ref[...]).astype(_BF16)
    f = p[:, 2 * D_C:].astype(_BF16)

    head_cols = []
    for hd in range(H_C):
        lo = hd * CHUNK
        rhs = jnp.concatenate([v[n * CHUNK:(n + 1) * CHUNK, lo:lo + CHUNK] for n in range(n_chunks)], axis=1)
        sv = _dot(ws_ref[hd], rhs)
        bias = bs_ref[hd]
        head_cols.append(jnp.concatenate(
            [sv[:, n * CHUNK:(n + 1) * CHUNK] + bias for n in range(n_chunks)], axis=0))
    y_c = (u * jnp.concatenate(head_cols, axis=1)).astype(_BF16)
    o_ref[...] = x + g1 * _dot(y_c, woutc_ref[...])

    z = _dot(f, cdft_ref[...])
    for part in range(2):
        for half in range(2):
            lo = (2 * part + half) * LANES
            z_ref[part, half] = z[:, lo:lo + LANES]


def _mix_cd_call(x, mod, b_off, ng, w_in, ln_g, ln_b, w_s, b_s_rows, w_out_c, cdft):
    bsz, seq, d = x.shape
    tile = CD_TOKEN_TILE
    n_in = w_in.shape[1]
    return pl.pallas_call(
        _mix_cd_body,
        out_shape=(jax.ShapeDtypeStruct(x.shape, _F32),
                   jax.ShapeDtypeStruct((bsz, 2, 2, seq, LANES), _F32)),
        grid=(bsz, seq // tile),
        in_specs=[
            pl.BlockSpec((None, tile, d), lambda b, t: (b, t, 0)),
            pl.BlockSpec((None, 1, N_MOD * d), lambda b, t: (b + b_off, 0, 0)),
            _const_spec((1, d)),
            _const_spec((d, n_in)),
            _const_spec((1, D_C)),
            _const_spec((1, D_C)),
            _const_spec((H_C, CHUNK, CHUNK)),
            _const_spec((H_C, CHUNK, CHUNK)),
            _const_spec((D_C, d)),
            _const_spec((D_D, 2 * D_D)),
        ],
        out_specs=(pl.BlockSpec((None, tile, d), lambda b, t: (b, t, 0)),
                   pl.BlockSpec((None, 2, 2, tile, LANES), lambda b, t: (b, 0, 0, t, 0))),
        compiler_params=_params(2),
        name="mixer_cd",
    )(x, mod, ng, w_in, ln_g, ln_b, w_s, b_s_rows, w_out_c, cdft)


def _seq_dft_body(z_ref, w1_ref, m_ref, y_ref, a_scr, g_scr, s_scr):
    step = pl.program_id(2)
    n1 = z_ref.shape[1]
    n2 = a_scr.shape[2]
    n_stage1 = n2 // SUBLANES

    @pl.when(step < n_stage1)
    def _stage1():
        for part in range(2):
            g_scr[part] = z_ref[part].reshape(n1 * SUBLANES, LANES)
        cols = [jnp.concatenate([g_scr[part, pl.ds(i, n1, stride=SUBLANES), :] for part in range(2)], axis=0)
                for i in range(SUBLANES)]
        rhs = jnp.concatenate(cols, axis=1).astype(_BF16)
        a = _dot(w1_ref[...], rhs)
        for part in range(2):
            for i in range(SUBLANES):
                g_scr[part, pl.ds(i, n1, stride=SUBLANES), :] = a[part * n1:(part + 1) * n1, i * LANES:(i + 1) * LANES]
        col0 = pl.multiple_of(step * SUBLANES, SUBLANES)
        for part in range(2):
            a_scr[part, :, pl.ds(col0, SUBLANES), :] = g_scr[part].reshape(n1, SUBLANES, LANES)

    @pl.when(step >= n_stage1)
    def _stage2():
        k0 = (step - n_stage1) * SUBLANES
        for i in range(SUBLANES):
            rhs = jnp.concatenate([a_scr[0, k0 + i], a_scr[1, k0 + i]], axis=0).astype(_BF16)
            s_scr[pl.ds(i, n2, stride=SUBLANES), :] = _dot(m_ref[i], rhs)
        y_ref[...] = s_scr[...].reshape(n2, SUBLANES, LANES)


def _seq_dft_call(z, w1, m2):
    bsz, _, _, seq, _ = z.shape
    n2 = DFT_N2
    n1 = seq // n2
    n_stage1 = n2 // SUBLANES
    n_stage2 = n1 // SUBLANES
    z6 = z.reshape(bsz, 2, 2, n1, n2, LANES)
    y = pl.pallas_call(
        _seq_dft_body,
        out_shape=jax.ShapeDtypeStruct((bsz, 2, n2, n1, LANES), _F32),
        grid=(bsz, 2, n_stage1 + n_stage2),
        in_specs=[
            pl.BlockSpec((None, 2, None, n1, SUBLANES, LANES),
                         lambda b, hf, st: (b, 0, hf, 0, jnp.minimum(st, n_stage1 - 1), 0)),
            _const_spec((2 * n1, 2 * n1)),
            pl.BlockSpec((SUBLANES, n2, 2 * n2), lambda b, hf, st: (jnp.maximum(st - n_stage1, 0), 0, 0)),
        ],
        out_specs=pl.BlockSpec((None, None, n2, SUBLANES, LANES),
                               lambda b, hf, st: (b, hf, 0, jnp.maximum(st - n_stage1, 0), 0)),
        scratch_shapes=[pltpu.VMEM((2, n1, n2, LANES), _F32),
                        pltpu.VMEM((2, n1 * SUBLANES, LANES), _F32),
                        pltpu.VMEM((n2 * SUBLANES, LANES), _F32)],
        compiler_params=pltpu.CompilerParams(
            dimension_semantics=("parallel", "parallel", "arbitrary"), vmem_limit_bytes=VMEM_LIMIT),
        name="seq_dft",
    )(z6, w1, m2)
    return y.reshape(bsz, 2, seq, LANES)


def _channel_dft_matrix():
    c = np.arange(D_D)
    same = (c[:, None] // DG_D) == (c[None, :] // DG_D)
    ang = 2.0 * np.pi * ((c[:, None] % DG_D) * (c[None, :] % DG_D) % DG_D) / DG_D
    scale = 1.0 / np.sqrt(DG_D)
    re = np.where(same, np.cos(ang), 0.0) * scale
    im = np.where(same, -np.sin(ang), 0.0) * scale
    return np.concatenate([re, im], axis=1)


def _stage1_matrix(n1):
    k = np.arange(n1)
    ang = 2.0 * np.pi * ((k[:, None] * k[None, :]) % n1) / n1
    c, s = np.cos(ang), np.sin(ang)
    return np.block([[c, s], [-s, c]])


def _stage2_matrices(n1, n2):
    seq = n1 * n2
    k1 = np.arange(n1)[:, None, None]
    k2 = np.arange(n2)[None, :, None]
    nn = np.arange(n2)[None, None, :]
    ang = 2.0 * np.pi * ((nn * (k1 + n1 * k2)) % seq) / seq
    scale = 1.0 / np.sqrt(seq)
    return np.concatenate([np.cos(ang), np.sin(ang)], axis=2) * scale


def _trunk(x, b_off, mod, wts):
    seq = x.shape[1]
    n1 = seq // DFT_N2
    x = _layer0_call(x, mod[0], b_off, wts["mix_g"][0], wts["ab_w_in"], wts["ab_conv_a"], wts["ab_conv_b_w"],
                     wts["ab_conv_b_b"], wts["ab_ln_g"], wts["ab_ln_b"], wts["ab_w_out"],
                     wts["ffn_g"][0], wts["ffn_w_in"][0], wts["ffn_w_out"][0])
    x, z = _mix_cd_call(x, mod[1], b_off, wts["mix_g"][1], wts["cd_w_in"], wts["cd_ln_g"], wts["cd_ln_b"],
                        wts["cd_w_s"], wts["cd_b_s_rows"], wts["cd_w_out_c"], wts["cdft"])
    yd = _seq_dft_call(z, _table(_stage1_matrix(n1)), _table(_stage2_matrices(n1, DFT_N2)))
    return _ffn_final_call(x, yd, mod[1], b_off, wts["ffn_g"][1], wts["ffn_w_in"][1], wts["ffn_w_out"][1],
                           wts["cd_w_out_d"], wts["final_g"])


def _table(values):
    return jnp.asarray(values, _F32).astype(_BF16)


def _mix_in_columns():
    half = MXU_DIM
    cols = []
    for first, second in ((3 * D_A, 3 * D_A + D_B), (D_A, 2 * D_A)):
        for lo in range(0, D_B, half):
            cols += list(range(first + lo, first + lo + half)) + list(range(second + lo, second + lo + half))
    return np.asarray(cols + list(range(0, D_A)), np.int32)


def _tap_rows(w):
    return jnp.broadcast_to(w[:, None, :], (w.shape[0], SUBLANES, w.shape[1]))


def kernel(x_prompt, x_sample, c_prompt, c_sample, ada_w, ada_b, mix_norm_g, ffn_norm_g, ab_w_in, ab_conv_a,
           ab_conv_b_w, ab_conv_b_b, ab_ln_g, ab_ln_b, ab_w_out, cd_w_in, cd_ln_g, cd_ln_b, cd_w_s, cd_b_s,
           cd_w_out, ffn_w_in, ffn_w_out, final_g):
    depth = ada_w.shape[0]
    assert depth == 2 and ab_w_in.shape[0] == 1 and cd_w_in.shape[0] == 1
    n_prompt, n_sample = c_prompt.shape[0], c_sample.shape[0]
    assert n_prompt + n_sample <= MOD_ROWS

    c_all = jnp.concatenate(
        [c_prompt, c_sample, jnp.zeros((MOD_ROWS - n_prompt - n_sample, D_MODEL), _F32)], axis=0)
    mod = _mod_call(c_all, ada_w, ada_b).reshape(depth, MOD_ROWS, 1, N_MOD * D_MODEL)

    wts = {
        "mix_g": mix_norm_g.reshape(depth, 1, D_MODEL),
        "ffn_g": ffn_norm_g.reshape(depth, 1, D_MODEL),
        "ab_w_in": ab_w_in[0][:, _mix_in_columns()].astype(_BF16),
        "ab_conv_a": _tap_rows(ab_conv_a[0]),
        "ab_conv_b_w": _tap_rows(ab_conv_b_w[0]),
        "ab_conv_b_b": ab_conv_b_b,
        "ab_ln_g": ab_ln_g,
        "ab_ln_b": ab_ln_b,
        "ab_w_out": ab_w_out[0].astype(_BF16),
        "cd_w_in": cd_w_in[0].astype(_BF16),
        "cd_ln_g": cd_ln_g,
        "cd_ln_b": cd_ln_b,
        "cd_w_s": cd_w_s[0].astype(_BF16),
        "cd_b_s_rows": jnp.broadcast_to(cd_b_s[0][:, :, None], (H_C, CHUNK, CHUNK)),
        "cd_w_out_c": cd_w_out[0, :D_C].astype(_BF16),
        "cd_w_out_d": cd_w_out[0, D_C:].astype(_BF16),
        "cdft": _table(_channel_dft_matrix()),
        "ffn_w_in": ffn_w_in.astype(_BF16),
        "ffn_w_out": ffn_w_out.astype(_BF16),
        "final_g": final_g.reshape(1, D_MODEL),
    }
    y_prompt = _trunk(x_prompt, 0, mod, wts)
    y_sample = _trunk(x_sample, n_prompt, mod, wts)
    return (y_prompt, y_sample)
```

```python
import jax
import jax.numpy as jnp
import numpy as np
from jax import lax
from jax.experimental import pallas as pl
from jax.experimental.pallas import tpu as pltpu

D_MODEL = 1024
D_A = 512
D_B = 512
K_A = 3
K_B = 31
D_C = 768
H_C = 6
CHUNK = 128
D_D = 256
DG_D = 64
D_FF = 2816
N_MOD = 6
EPS = 1e-6

LANES = 128
SUBLANES = 8
MXU_DIM = 256
TOKEN_TILE = 512
WIDE_TOKEN_TILE = 1024
SUB_TILES = 2
HALO = 16
CONV_STRIDE = 4
CONV_GROUP = SUBLANES * CONV_STRIDE
CONV_ACCS = 8
FFN_SPLIT = 6 * MXU_DIM
DFT_BLOCK_ROWS = 4096
DFT_N2 = 128
MOD_ROWS = 16
MOD_COLS = 1536
VMEM_LIMIT = 56 * 1024 * 1024

_BF16 = jnp.bfloat16
_F32 = jnp.float32


def _const_spec(shape):
    nd = len(shape)
    return pl.BlockSpec(shape, lambda *_: (0,) * nd, pipeline_mode=pl.Buffered(1))


def _layer_spec(shape, layer):
    nd = len(shape)
    return pl.BlockSpec((None,) + shape, lambda *_: (layer,) + (0,) * nd, pipeline_mode=pl.Buffered(1))


def _params(n_grid):
    return pltpu.CompilerParams(
        dimension_semantics=("parallel",) * n_grid, vmem_limit_bytes=VMEM_LIMIT)


def _dot(a, b):
    return jnp.dot(a, b, preferred_element_type=_F32)


def _modulated_rmsnorm(x, gain, scale, shift):
    ms = jnp.mean(x * x, axis=-1, keepdims=True)
    return x * lax.rsqrt(ms + EPS) * (gain * (1.0 + scale)) + shift


def _layernorm(x, g, b):
    mu = jnp.mean(x, axis=-1, keepdims=True)
    xc = x - mu
    var = jnp.mean(xc * xc, axis=-1, keepdims=True)
    return xc * lax.rsqrt(var + EPS) * g + b


def _mod_slices(mod):
    d = D_MODEL
    return [mod[:, i * d:(i + 1) * d] for i in range(N_MOD)]


def _swiglu_residual(x, mod, ng_ref, win_ref, wout_ref):
    _, _, _, sh2, sc2, g2 = _mod_slices(mod)
    h = _modulated_rmsnorm(x, ng_ref[...], sc2, sh2).astype(_BF16)
    acc = None
    for lo, hi in ((0, FFN_SPLIT), (FFN_SPLIT, D_FF)):
        gate = _dot(h, win_ref[:, lo:hi])
        up = _dot(h, win_ref[:, D_FF + lo:D_FF + hi])
        a = (gate * jax.nn.sigmoid(gate) * up).astype(_BF16)
        part = _dot(a, wout_ref[lo:hi, :])
        acc = part if acc is None else acc + part
    return x + g2 * acc


def _mod_body(c_ref, w_ref, b_ref, o_ref):
    c = c_ref[...]
    sc = (c * jax.nn.sigmoid(c)).astype(_BF16)
    o_ref[...] = _dot(sc, w_ref[...].astype(_BF16)) + b_ref[...]


def _mod_call(c_all, ada_w, ada_b):
    depth = ada_w.shape[0]
    n_out = ada_w.shape[2]
    return pl.pallas_call(
        _mod_body,
        out_shape=jax.ShapeDtypeStruct((depth, MOD_ROWS, n_out), _F32),
        grid=(depth, n_out // MOD_COLS),
        in_specs=[
            pl.BlockSpec((MOD_ROWS, D_MODEL), lambda l, n: (0, 0)),
            pl.BlockSpec((None, D_MODEL, MOD_COLS), lambda l, n: (l, 0, n)),
            pl.BlockSpec((None, 1, MOD_COLS), lambda l, n: (l, 0, n)),
        ],
        out_specs=pl.BlockSpec((None, MOD_ROWS, MOD_COLS), lambda l, n: (l, 0, n)),
        compiler_params=_params(2),
        name="adaln_mod",
    )(c_all, ada_w, ada_b.reshape(depth, 1, n_out))


def _dwconv(src_ref, w_ref, first_row, tokens, dst_ref, slabs):
    n_taps = w_ref.shape[0]
    for c in slabs:
        lanes = slice(c * LANES, (c + 1) * LANES)
        for v0 in range(tokens.start // SUBLANES, tokens.stop // SUBLANES, CONV_ACCS):
            starts = [CONV_GROUP * (v // CONV_STRIDE) + v % CONV_STRIDE for v in range(v0, v0 + CONV_ACCS)]
            accs = [None] * CONV_ACCS
            for k in range(n_taps):
                w = w_ref[k, :, lanes]
                for q, t0 in enumerate(starts):
                    term = w * src_ref[c, pl.ds(first_row + k + t0, SUBLANES, stride=CONV_STRIDE), :]
                    accs[q] = term if accs[q] is None else accs[q] + term
            for q, t0 in enumerate(starts):
                dst_ref[c, pl.ds(t0, SUBLANES, stride=CONV_STRIDE), :] = accs[q]


def _after(x, dep, zeros_ref):
    bits = pltpu.bitcast(dep, jnp.int32)
    acc = None
    for r in range(0, dep.shape[0], SUBLANES):
        for c in range(0, dep.shape[1], LANES):
            blk = bits[r:r + SUBLANES, c:c + LANES]
            acc = blk if acc is None else acc | blk
    z = pltpu.bitcast(acc & zeros_ref[...], _F32)
    row = jnp.concatenate([z] * (x.shape[1] // LANES), axis=1)
    return x + jnp.concatenate([row] * (x.shape[0] // SUBLANES), axis=0)


def _layer0_body(xp_ref, x_ref, xn_ref, mod_ref, ng_ref, win_ref, ca_ref, cbw_ref, cbb_ref,
                 lng_ref, lnb_ref, wout_ref, fg_ref, fwin_ref, fwout_ref, zeros_ref, slot_ref, o_ref,
                 p_scr, u_scr, g_scr, ca_scr, cb_scr):
    t = pl.program_id(1)
    nt = pl.num_programs(1)
    tile = x_ref.shape[0]
    sub = tile // SUB_TILES
    half = MXU_DIM
    slabs_per_half = half // LANES
    mod = mod_ref[...]
    sh1, sc1, g1, _, _, _ = _mod_slices(mod)

    xin = jnp.concatenate([xp_ref[...], x_ref[...], xn_ref[...]], axis=0)
    h = _modulated_rmsnorm(xin, ng_ref[...], sc1, sh1).astype(_BF16)
    all_rows = tile + 2 * HALO
    r = lax.broadcasted_iota(jnp.int32, (all_rows, 1), 0)
    valid = jnp.logical_and(jnp.logical_or(r >= HALO, t > 0),
                            jnp.logical_or(r < HALO + tile, t < nt - 1))

    n_in = win_ref.shape[1]
    slot = slot_ref[0]
    for lo in range(0, n_in - D_A, 2 * half):
        p_scr[slot, :, lo:lo + 2 * half] = _dot(h, win_ref[:, lo:lo + 2 * half])
    p_scr[slot, HALO:HALO + tile, n_in - D_A:] = _dot(h[HALO:HALO + tile, :], win_ref[:, n_in - D_A:])

    for j in range(D_B // half):
        q = p_scr[slot, :, 2 * half * j:2 * half * (j + 1)]
        g = jnp.where(valid, q[:, :half] * jax.nn.sigmoid(q[:, half:]), 0.0)
        slabs = range(slabs_per_half * j, slabs_per_half * (j + 1))
        for c in slabs:
            lo = (c - slabs[0]) * LANES
            g_scr[c] = g[:, lo:lo + LANES]
        _dwconv(g_scr, cbw_ref, HALO - K_B // 2, range(0, sub), cb_scr, slabs)
    for j in range(D_A // half):
        base = 2 * D_B + 2 * half * j
        q = p_scr[slot, :, base:base + 2 * half]
        u = jnp.where(valid, q[:, :half] * q[:, half:], 0.0)
        slabs = range(slabs_per_half * j, slabs_per_half * (j + 1))
        for c in slabs:
            lo = (c - slabs[0]) * LANES
            u_scr[c] = u[:, lo:lo + LANES]
        _dwconv(u_scr, ca_ref, HALO - K_A // 2, range(0, sub), ca_scr, slabs)
    a_bs = [p_scr[slot, HALO + s * sub:HALO + (s + 1) * sub, n_in - D_A:] for s in range(SUB_TILES)]
    _dwconv(g_scr, cbw_ref, HALO - K_B // 2, range(sub, tile), cb_scr, range(D_B // LANES))
    _dwconv(u_scr, ca_ref, HALO - K_A // 2, range(sub, tile), ca_scr, range(D_A // LANES))

    prev_out = None
    for s in range(SUB_TILES):
        tok = slice(s * sub, (s + 1) * sub)
        y_a = a_bs[s] * jnp.concatenate([ca_scr[c, tok, :] for c in range(D_A // LANES)], axis=1)
        z = _layernorm(jnp.concatenate([cb_scr[c, tok, :] for c in range(D_B // LANES)], axis=1) + cbb_ref[...],
                       lng_ref[...], lnb_ref[...])
        y_b = z * jax.nn.sigmoid(z)
        gate_dep = a_bs[-1][sub - SUBLANES:, :]
        y_a = _after(y_a, gate_dep, zeros_ref)
        y_b = _after(y_b, gate_dep, zeros_ref)

        m = _dot(y_a.astype(_BF16), wout_ref[0:D_A, :]) + _dot(y_b.astype(_BF16), wout_ref[D_A:, :])
        x1 = x_ref[s * sub:(s + 1) * sub, :] + g1 * m
        if prev_out is not None:
            x1 = _after(x1, prev_out[sub - SUBLANES:, :], zeros_ref)
        prev_out = _swiglu_residual(x1, mod, fg_ref, fwin_ref, fwout_ref)
        o_ref[s * sub:(s + 1) * sub, :] = prev_out


def _layer0_call(x, mod, b_off, ng, w_in, conv_a, conv_b_w, conv_b_b, ln_g, ln_b, w_out, fg, f_w_in, f_w_out):
    bsz, seq, d = x.shape
    tile = TOKEN_TILE
    per = tile // HALO
    n_halo_blocks = seq // HALO
    n_in = w_in.shape[1]
    return pl.pallas_call(
        _layer0_body,
        out_shape=jax.ShapeDtypeStruct(x.shape, _F32),
        grid=(bsz, seq // tile),
        in_specs=[
            pl.BlockSpec((None, HALO, d), lambda b, t: (b, jnp.maximum(t * per - 1, 0), 0)),
            pl.BlockSpec((None, tile, d), lambda b, t: (b, t, 0)),
            pl.BlockSpec((None, HALO, d), lambda b, t: (b, jnp.minimum((t + 1) * per, n_halo_blocks - 1), 0)),
            pl.BlockSpec((None, 1, N_MOD * d), lambda b, t: (b + b_off, 0, 0)),
            _const_spec((1, d)),
            _const_spec((d, n_in)),
            _const_spec((K_A, SUBLANES, D_A)),
            _const_spec((K_B, SUBLANES, D_B)),
            _const_spec((1, D_B)),
            _const_spec((1, D_B)),
            _const_spec((1, D_B)),
            _const_spec((D_A + D_B, d)),
            _const_spec((1, d)),
            _layer_spec((d, 2 * D_FF), 0),
            _layer_spec((D_FF, d), 0),
            _const_spec((SUBLANES, LANES)),
            pl.BlockSpec(memory_space=pltpu.SMEM),
        ],
        out_specs=pl.BlockSpec((None, tile, d), lambda b, t: (b, t, 0)),
        scratch_shapes=[pltpu.VMEM((1, tile + 2 * HALO, n_in), _F32),
                        pltpu.VMEM((D_A // LANES, tile + 2 * HALO, LANES), _F32),
                        pltpu.VMEM((D_B // LANES, tile + 2 * HALO, LANES), _F32),
                        pltpu.VMEM((D_A // LANES, tile, LANES), _F32),
                        pltpu.VMEM((D_B // LANES, tile, LANES), _F32)],
        compiler_params=_params(2),
        name="layer0",
    )(x, x, x, mod, ng, w_in, conv_a, conv_b_w, conv_b_b, ln_g, ln_b, w_out, fg, f_w_in, f_w_out,
      jnp.zeros((SUBLANES, LANES), jnp.int32), jnp.zeros((1,), jnp.int32))


def _ffn_final_body(x_ref, yd_ref, mod_ref, ng_ref, win_ref, wout_ref, wd_ref, fg_ref, o_ref):
    mod = mod_ref[...]
    g1 = mod[:, 2 * D_MODEL:3 * D_MODEL]
    yd = jnp.concatenate([yd_ref[0], yd_ref[1]], axis=1).astype(_BF16)
    x = x_ref[...] + g1 * _dot(yd, wd_ref[...])
    x = _swiglu_residual(x, mod, ng_ref, win_ref, wout_ref)
    ms = jnp.mean(x * x, axis=-1, keepdims=True)
    o_ref[...] = x * lax.rsqrt(ms + EPS) * fg_ref[...]


def _ffn_final_call(x, yd, mod, b_off, ng, w_in, w_out, w_d, final_g):
    bsz, seq, d = x.shape
    tile = WIDE_TOKEN_TILE
    return pl.pallas_call(
        _ffn_final_body,
        out_shape=jax.ShapeDtypeStruct(x.shape, _F32),
        grid=(bsz, seq // tile),
        in_specs=[
            pl.BlockSpec((None, tile, d), lambda b, t: (b, t, 0)),
            pl.BlockSpec((None, 2, tile, LANES), lambda b, t: (b, 0, t, 0)),
            pl.BlockSpec((None, 1, N_MOD * d), lambda b, t: (b + b_off, 0, 0)),
            _const_spec((1, d)),
            _layer_spec((d, 2 * D_FF), 1),
            _layer_spec((D_FF, d), 1),
            _const_spec((D_D, d)),
            _const_spec((1, d)),
        ],
        out_specs=pl.BlockSpec((None, tile, d), lambda b, t: (b, t, 0)),
        compiler_params=_params(2),
        name="ffn_final",
    )(x, yd, mod, ng, w_in, w_out, w_d, final_g)


def _mix_cd_body(x_ref, mod_ref, ng_ref, win_ref, lng_ref, lnb_ref, ws_ref, bs_ref, woutc_ref,
                 cdft_ref, o_ref, z_ref):
    tile = x_ref.shape[0]
    n_chunks = tile // CHUNK
    x = x_ref[...]
    sh1, sc1, g1, _, _, _ = _mod_slices(mod_ref[...])
    h = _modulated_rmsnorm(x, ng_ref[...], sc1, sh1).astype(_BF16)
    p = _dot(h, win_ref[...])
    u = p[:, 0:D_C]
    v = _layernorm(p[:, D_C:2 * D_C], lng_ref[...], lnb_ref[...]).astype(_BF16)
    f = p[:, 2 * D_C:].astype(_BF16)

    head_cols = []
    for hd in range(H_C):
        lo = hd * CHUNK
        rhs = jnp.concatenate([v[n * CHUNK:(n + 1) * CHUNK, lo:lo + CHUNK] for n in range(n_chunks)], axis=1)
        sv = _dot(ws_ref[hd], rhs)
        bias = bs_ref[hd]
        head_cols.append(jnp.concatenate(
            [sv[:, n * CHUNK:(n + 1) * CHUNK] + bias for n in range(n_chunks)], axis=0))
    y_c = (u * jnp.concatenate(head_cols, axis=1)).astype(_BF16)
    o_ref[...] = x + g1 * _dot(y_c, woutc_ref[...])

    z = _dot(f, cdft_ref[...])
    for part in range(2):
        for half in range(2):
            lo = (2 * part + half) * LANES
            z_ref[part, half] = z[:, lo:lo + LANES]


def _mix_cd_call(x, mod, b_off, ng, w_in, ln_g, ln_b, w_s, b_s_rows, w_out_c, cdft):
    bsz, seq, d = x.shape
    tile = WIDE_TOKEN_TILE
    n_in = w_in.shape[1]
    return pl.pallas_call(
        _mix_cd_body,
        out_shape=(jax.ShapeDtypeStruct(x.shape, _F32),
                   jax.ShapeDtypeStruct((bsz, 2, 2, seq, LANES), _F32)),
        grid=(bsz, seq // tile),
        in_specs=[
            pl.BlockSpec((None, tile, d), lambda b, t: (b, t, 0)),
            pl.BlockSpec((None, 1, N_MOD * d), lambda b, t: (b + b_off, 0, 0)),
            _const_spec((1, d)),
            _const_spec((d, n_in)),
            _const_spec((1, D_C)),
            _const_spec((1, D_C)),
            _const_spec((H_C, CHUNK, CHUNK)),
            _const_spec((H_C, CHUNK, CHUNK)),
            _const_spec((D_C, d)),
            _const_spec((D_D, 2 * D_D)),
        ],
        out_specs=(pl.BlockSpec((None, tile, d), lambda b, t: (b, t, 0)),
                   pl.BlockSpec((None, 2, 2, tile, LANES), lambda b, t: (b, 0, 0, t, 0))),
        compiler_params=_params(2),
        name="mixer_cd",
    )(x, mod, ng, w_in, ln_g, ln_b, w_s, b_s_rows, w_out_c, cdft)


def _seq_dft_body(z_ref, w1_ref, m_ref, y_ref, a_scr, g_scr, s_scr):
    step = pl.program_id(2)
    n1 = z_ref.shape[1]
    n2 = a_scr.shape[2]
    per_step = z_ref.shape[2]
    n_stage1 = n2 // per_step

    @pl.when(step < n_stage1)
    def _stage1():
        for grp in range(per_step // SUBLANES):
            rows = slice(grp * SUBLANES, (grp + 1) * SUBLANES)
            for part in range(2):
                g_scr[grp, part] = z_ref[part, :, rows, :].reshape(n1 * SUBLANES, LANES)
            cols = [jnp.concatenate([g_scr[grp, part, pl.ds(i, n1, stride=SUBLANES), :] for part in range(2)], axis=0)
                    for i in range(SUBLANES)]
            rhs = jnp.concatenate(cols, axis=1).astype(_BF16)
            a = _dot(w1_ref[...], rhs)
            for part in range(2):
                for i in range(SUBLANES):
                    g_scr[grp, part, pl.ds(i, n1, stride=SUBLANES), :] = (
                        a[part * n1:(part + 1) * n1, i * LANES:(i + 1) * LANES])
            col0 = pl.multiple_of(step * per_step + grp * SUBLANES, SUBLANES)
            for part in range(2):
                a_scr[part, :, pl.ds(col0, SUBLANES), :] = g_scr[grp, part].reshape(n1, SUBLANES, LANES)

    @pl.when(step >= n_stage1)
    def _stage2():
        k0 = (step - n_stage1) * per_step
        for grp in range(per_step // SUBLANES):
            for i in range(SUBLANES):
                k1 = grp * SUBLANES + i
                rhs = jnp.concatenate([a_scr[0, k0 + k1], a_scr[1, k0 + k1]], axis=0).astype(_BF16)
                s_scr[grp, pl.ds(i, n2, stride=SUBLANES), :] = _dot(m_ref[k1], rhs)
            y_ref[:, grp * SUBLANES:(grp + 1) * SUBLANES, :] = s_scr[grp].reshape(n2, SUBLANES, LANES)


def _seq_dft_call(z, w1, m2):
    bsz, _, _, seq, _ = z.shape
    n2 = DFT_N2
    n1 = seq // n2
    per_step = DFT_BLOCK_ROWS // n1
    n_stage1 = n2 // per_step
    n_stage2 = n1 // per_step
    groups = per_step // SUBLANES
    z6 = z.reshape(bsz, 2, 2, n1, n2, LANES)
    y = pl.pallas_call(
        _seq_dft_body,
        out_shape=jax.ShapeDtypeStruct((bsz, 2, n2, n1, LANES), _F32),
        grid=(bsz, 2, n_stage1 + n_stage2),
        in_specs=[
            pl.BlockSpec((None, 2, None, n1, per_step, LANES),
                         lambda b, hf, st: (b, 0, hf, 0, jnp.minimum(st, n_stage1 - 1), 0)),
            _const_spec((2 * n1, 2 * n1)),
            pl.BlockSpec((per_step, n2, 2 * n2), lambda b, hf, st: (jnp.maximum(st - n_stage1, 0), 0, 0)),
        ],
        out_specs=pl.BlockSpec((None, None, n2, per_step, LANES),
                               lambda b, hf, st: (b, hf, 0, jnp.maximum(st - n_stage1, 0), 0)),
        scratch_shapes=[pltpu.VMEM((2, n1, n2, LANES), _F32),
                        pltpu.VMEM((groups, 2, n1 * SUBLANES, LANES), _F32),
                        pltpu.VMEM((groups, n2 * SUBLANES, LANES), _F32)],
        compiler_params=pltpu.CompilerParams(
            dimension_semantics=("parallel", "parallel", "arbitrary"), vmem_limit_bytes=VMEM_LIMIT),
        name="seq_dft",
    )(z6, w1, m2)
    return y.reshape(bsz, 2, seq, LANES)


def _channel_dft_matrix():
    c = np.arange(D_D)
    same = (c[:, None] // DG_D) == (c[None, :] // DG_D)
    ang = 2.0 * np.pi * ((c[:, None] % DG_D) * (c[None, :] % DG_D) % DG_D) / DG_D
    scale = 1.0 / np.sqrt(DG_D)
    re = np.where(same, np.cos(ang), 0.0) * scale
    im = np.where(same, -np.sin(ang), 0.0) * scale
    return np.concatenate([re, im], axis=1)


def _stage1_matrix(n1):
    k = np.arange(n1)
    ang = 2.0 * np.pi * ((k[:, None] * k[None, :]) % n1) / n1
    c, s = np.cos(ang), np.sin(ang)
    return np.block([[c, s], [-s, c]])


def _stage2_matrices(n1, n2):
    seq = n1 * n2
    k1 = np.arange(n1)[:, None, None]
    k2 = np.arange(n2)[None, :, None]
    nn = np.arange(n2)[None, None, :]
    ang = 2.0 * np.pi * ((nn * (k1 + n1 * k2)) % seq) / seq
    scale = 1.0 / np.sqrt(seq)
    return np.concatenate([np.cos(ang), np.sin(ang)], axis=2) * scale


def _trunk(x, b_off, mod, wts):
    seq = x.shape[1]
    n1 = seq // DFT_N2
    x = _layer0_call(x, mod[0], b_off, wts["mix_g"][0], wts["ab_w_in"], wts["ab_conv_a"], wts["ab_conv_b_w"],
                     wts["ab_conv_b_b"], wts["ab_ln_g"], wts["ab_ln_b"], wts["ab_w_out"],
                     wts["ffn_g"][0], wts["ffn_w_in"], wts["ffn_w_out"])
    x, z = _mix_cd_call(x, mod[1], b_off, wts["mix_g"][1], wts["cd_w_in"], wts["cd_ln_g"], wts["cd_ln_b"],
                        wts["cd_w_s"], wts["cd_b_s_rows"], wts["cd_w_out_c"], wts["cdft"])
    yd = _seq_dft_call(z, _table(_stage1_matrix(n1)), _table(_stage2_matrices(n1, DFT_N2)))
    return _ffn_final_call(x, yd, mod[1], b_off, wts["ffn_g"][1], wts["ffn_w_in"], wts["ffn_w_out"],
                           wts["cd_w_out_d"], wts["final_g"])


def _table(values):
    return jnp.asarray(values, _F32).astype(_BF16)


def _mix_in_columns():
    half = MXU_DIM
    cols = []
    for first, second in ((3 * D_A, 3 * D_A + D_B), (D_A, 2 * D_A)):
        for lo in range(0, D_B, half):
            cols += list(range(first + lo, first + lo + half)) + list(range(second + lo, second + lo + half))
    return np.asarray(cols + list(range(0, D_A)), np.int32)


def _tap_rows(w):
    return jnp.broadcast_to(w[:, None, :], (w.shape[0], SUBLANES, w.shape[1]))


def kernel(x_prompt, x_sample, c_prompt, c_sample, ada_w, ada_b, mix_norm_g, ffn_norm_g, ab_w_in, ab_conv_a,
           ab_conv_b_w, ab_conv_b_b, ab_ln_g, ab_ln_b, ab_w_out, cd_w_in, cd_ln_g, cd_ln_b, cd_w_s, cd_b_s,
           cd_w_out, ffn_w_in, ffn_w_out, final_g):
    depth = ada_w.shape[0]
    assert depth == 2 and ab_w_in.shape[0] == 1 and cd_w_in.shape[0] == 1
    n_prompt, n_sample = c_prompt.shape[0], c_sample.shape[0]
    assert n_prompt + n_sample <= MOD_ROWS

    c_all = jnp.concatenate(
        [c_prompt, c_sample, jnp.zeros((MOD_ROWS - n_prompt - n_sample, D_MODEL), _F32)], axis=0)
    mod = _mod_call(c_all, ada_w, ada_b).reshape(depth, MOD_ROWS, 1, N_MOD * D_MODEL)

    wts = {
        "mix_g": mix_norm_g.reshape(depth, 1, D_MODEL),
        "ffn_g": ffn_norm_g.reshape(depth, 1, D_MODEL),
        "ab_w_in": ab_w_in[0].astype(_BF16)[:, _mix_in_columns()],
        "ab_conv_a": _tap_rows(ab_conv_a[0]),
        "ab_conv_b_w": _tap_rows(ab_conv_b_w[0]),
        "ab_conv_b_b": ab_conv_b_b,
        "ab_ln_g": ab_ln_g,
        "ab_ln_b": ab_ln_b,
        "ab_w_out": ab_w_out[0].astype(_BF16),
        "cd_w_in": cd_w_in[0].astype(_BF16),
        "cd_ln_g": cd_ln_g,
        "cd_ln_b": cd_ln_b,
        "cd_w_s": cd_w_s[0].astype(_BF16),
        "cd_b_s_rows": jnp.broadcast_to(cd_b_s[0][:, :, None], (H_C, CHUNK, CHUNK)),
        "cd_w_out_c": cd_w_out[0, :D_C].astype(_BF16),
        "cd_w_out_d": cd_w_out[0, D_C:].astype(_BF16),
        "cdft": _table(_channel_dft_matrix()),
        "ffn_w_in": ffn_w_in.astype(_BF16),
        "ffn_w_out": ffn_w_out.astype(_BF16),
        "final_g": final_g.reshape(1, D_MODEL),
    }
    y_prompt = _trunk(x_prompt, 0, mod, wts)
    y_sample = _trunk(x_sample, n_prompt, mod, wts)
    return (y_prompt, y_sample)
```

```python
import jax
import jax.numpy as jnp
import numpy as np
from jax import lax
from jax.experimental import pallas as pl
from jax.experimental.pallas import tpu as pltpu

D_MODEL = 1024
D_A = 512
D_B = 512
K_A = 3
K_B = 31
D_C = 768
H_C = 6
CHUNK = 128
D_D = 256
DG_D = 64
D_FF = 2816
N_MOD = 6
EPS = 1e-6

LANES = 128
SUBLANES = 8
MXU_DIM = 256
TOKEN_TILE = 512
WIDE_TOKEN_TILE = 1024
SUB_TILES = 2
HALO = 16
CONV_STRIDE = 4
CONV_GROUP = SUBLANES * CONV_STRIDE
CONV_ACCS = 8
FFN_SPLIT = 6 * MXU_DIM
DFT_BLOCK_ROWS = 4096
DFT_N2 = 128
MOD_ROWS = 16
MOD_COLS = 1536
VMEM_LIMIT = 56 * 1024 * 1024

_BF16 = jnp.bfloat16
_F32 = jnp.float32


def _const_spec(shape):
    nd = len(shape)
    return pl.BlockSpec(shape, lambda *_: (0,) * nd, pipeline_mode=pl.Buffered(1))


def _layer_spec(shape, layer):
    nd = len(shape)
    return pl.BlockSpec((None,) + shape, lambda *_: (layer,) + (0,) * nd, pipeline_mode=pl.Buffered(1))


def _params(n_grid):
    return pltpu.CompilerParams(
        dimension_semantics=("parallel",) * n_grid, vmem_limit_bytes=VMEM_LIMIT)


def _dot(a, b):
    return jnp.dot(a, b, preferred_element_type=_F32)


def _modulated_rmsnorm(x, gain, scale, shift):
    ms = jnp.mean(x * x, axis=-1, keepdims=True)
    return x * lax.rsqrt(ms + EPS) * (gain * (1.0 + scale)) + shift


def _layernorm(x, g, b):
    mu = jnp.mean(x, axis=-1, keepdims=True)
    xc = x - mu
    var = jnp.mean(xc * xc, axis=-1, keepdims=True)
    return xc * lax.rsqrt(var + EPS) * g + b


def _mod_slices(mod):
    d = D_MODEL
    return [mod[:, i * d:(i + 1) * d] for i in range(N_MOD)]


def _swiglu_residual(x, mod, ng_ref, win_ref, wout_ref):
    _, _, _, sh2, sc2, g2 = _mod_slices(mod)
    h = _modulated_rmsnorm(x, ng_ref[...], sc2, sh2).astype(_BF16)
    acc = None
    for lo, hi in ((0, FFN_SPLIT), (FFN_SPLIT, D_FF)):
        gate = _dot(h, win_ref[:, lo:hi])
        up = _dot(h, win_ref[:, D_FF + lo:D_FF + hi])
        a = (gate * jax.nn.sigmoid(gate) * up).astype(_BF16)
        part = _dot(a, wout_ref[lo:hi, :])
        acc = part if acc is None else acc + part
    return x + g2 * acc


def _mod_body(c_ref, w_ref, b_ref, o_ref):
    c = c_ref[...]
    sc = (c * jax.nn.sigmoid(c)).astype(_BF16)
    o_ref[...] = _dot(sc, w_ref[...].astype(_BF16)) + b_ref[...]


def _mod_call(c_all, ada_w, ada_b):
    depth = ada_w.shape[0]
    n_out = ada_w.shape[2]
    return pl.pallas_call(
        _mod_body,
        out_shape=jax.ShapeDtypeStruct((depth, MOD_ROWS, n_out), _F32),
        grid=(depth, n_out // MOD_COLS),
        in_specs=[
            pl.BlockSpec((MOD_ROWS, D_MODEL), lambda l, n: (0, 0)),
            pl.BlockSpec((None, D_MODEL, MOD_COLS), lambda l, n: (l, 0, n)),
            pl.BlockSpec((None, 1, MOD_COLS), lambda l, n: (l, 0, n)),
        ],
        out_specs=pl.BlockSpec((None, MOD_ROWS, MOD_COLS), lambda l, n: (l, 0, n)),
        compiler_params=_params(2),
        name="adaln_mod",
    )(c_all, ada_w, ada_b.reshape(depth, 1, n_out))


def _dwconv(src_ref, w_ref, first_row, tokens, dst_ref, slabs):
    n_taps = w_ref.shape[0]
    for c in slabs:
        lanes = slice(c * LANES, (c + 1) * LANES)
        for v0 in range(tokens.start // SUBLANES, tokens.stop // SUBLANES, CONV_ACCS):
            starts = [CONV_GROUP * (v // CONV_STRIDE) + v % CONV_STRIDE for v in range(v0, v0 + CONV_ACCS)]
            accs = [None] * CONV_ACCS
            for k in range(n_taps):
                w = w_ref[k, :, lanes]
                for q, t0 in enumerate(starts):
                    term = w * src_ref[c, pl.ds(first_row + k + t0, SUBLANES, stride=CONV_STRIDE), :]
                    accs[q] = term if accs[q] is None else accs[q] + term
            for q, t0 in enumerate(starts):
                dst_ref[c, pl.ds(t0, SUBLANES, stride=CONV_STRIDE), :] = accs[q]


def _after(x, dep, zeros_ref):
    bits = pltpu.bitcast(dep, jnp.int32)
    acc = None
    for r in range(0, dep.shape[0], SUBLANES):
        for c in range(0, dep.shape[1], LANES):
            blk = bits[r:r + SUBLANES, c:c + LANES]
            acc = blk if acc is None else acc | blk
    z = pltpu.bitcast(acc & zeros_ref[...], _F32)
    row = jnp.concatenate([z] * (x.shape[1] // LANES), axis=1)
    return x + jnp.concatenate([row] * (x.shape[0] // SUBLANES), axis=0)


def _layer0_body(xp_ref, x_ref, xn_ref, mod_ref, ng_ref, win_ref, ca_ref, cbw_ref, cbb_ref,
                 lng_ref, lnb_ref, wout_ref, fg_ref, fwin_ref, fwout_ref, zeros_ref, slot_ref, o_ref,
                 p_scr, u_scr, g_scr, ca_scr, cb_scr):
    t = pl.program_id(1)
    nt = pl.num_programs(1)
    tile = x_ref.shape[0]
    sub = tile // SUB_TILES
    half = MXU_DIM
    slabs_per_half = half // LANES
    mod = mod_ref[...]
    sh1, sc1, g1, _, _, _ = _mod_slices(mod)

    xin = jnp.concatenate([xp_ref[...], x_ref[...], xn_ref[...]], axis=0)
    h = _modulated_rmsnorm(xin, ng_ref[...], sc1, sh1).astype(_BF16)
    all_rows = tile + 2 * HALO
    r = lax.broadcasted_iota(jnp.int32, (all_rows, 1), 0)
    valid = jnp.logical_and(jnp.logical_or(r >= HALO, t > 0),
                            jnp.logical_or(r < HALO + tile, t < nt - 1))

    n_in = win_ref.shape[1]
    slot = slot_ref[0]
    dst = 0
    for first, second in ((3 * D_A, 3 * D_A + D_B), (D_A, 2 * D_A)):
        for lo in range(0, D_B, half):
            for src in (first + lo, second + lo):
                p_scr[slot, :, dst:dst + half] = _dot(h, win_ref[:, src:src + half])
                dst += half
    p_scr[slot, HALO:HALO + tile, dst:] = _dot(h[HALO:HALO + tile, :], win_ref[:, 0:D_A])

    for j in range(D_B // half):
        q = p_scr[slot, :, 2 * half * j:2 * half * (j + 1)]
        g = jnp.where(valid, q[:, :half] * jax.nn.sigmoid(q[:, half:]), 0.0)
        slabs = range(slabs_per_half * j, slabs_per_half * (j + 1))
        for c in slabs:
            lo = (c - slabs[0]) * LANES
            g_scr[c] = g[:, lo:lo + LANES]
        _dwconv(g_scr, cbw_ref, HALO - K_B // 2, range(0, sub), cb_scr, slabs)
    for j in range(D_A // half):
        base = 2 * D_B + 2 * half * j
        q = p_scr[slot, :, base:base + 2 * half]
        u = jnp.where(valid, q[:, :half] * q[:, half:], 0.0)
        slabs = range(slabs_per_half * j, slabs_per_half * (j + 1))
        for c in slabs:
            lo = (c - slabs[0]) * LANES
            u_scr[c] = u[:, lo:lo + LANES]
        _dwconv(u_scr, ca_ref, HALO - K_A // 2, range(0, sub), ca_scr, slabs)
    a_bs = [p_scr[slot, HALO + s * sub:HALO + (s + 1) * sub, n_in - D_A:] for s in range(SUB_TILES)]
    _dwconv(g_scr, cbw_ref, HALO - K_B // 2, range(sub, tile), cb_scr, range(D_B // LANES))
    _dwconv(u_scr, ca_ref, HALO - K_A // 2, range(sub, tile), ca_scr, range(D_A // LANES))

    prev_out = None
    for s in range(SUB_TILES):
        tok = slice(s * sub, (s + 1) * sub)
        y_a = a_bs[s] * jnp.concatenate([ca_scr[c, tok, :] for c in range(D_A // LANES)], axis=1)
        z = _layernorm(jnp.concatenate([cb_scr[c, tok, :] for c in range(D_B // LANES)], axis=1) + cbb_ref[...],
                       lng_ref[...], lnb_ref[...])
        y_b = z * jax.nn.sigmoid(z)
        gate_dep = a_bs[-1][sub - SUBLANES:, :]
        y_a = _after(y_a, gate_dep, zeros_ref)
        y_b = _after(y_b, gate_dep, zeros_ref)

        m = _dot(y_a.astype(_BF16), wout_ref[0:D_A, :]) + _dot(y_b.astype(_BF16), wout_ref[D_A:, :])
        x1 = x_ref[s * sub:(s + 1) * sub, :] + g1 * m
        if prev_out is not None:
            x1 = _after(x1, prev_out[sub - SUBLANES:, :], zeros_ref)
        prev_out = _swiglu_residual(x1, mod, fg_ref, fwin_ref, fwout_ref)
        o_ref[s * sub:(s + 1) * sub, :] = prev_out


def _layer0_call(x, mod, b_off, ng, w_in, conv_a, conv_b_w, conv_b_b, ln_g, ln_b, w_out, fg, f_w_in, f_w_out):
    bsz, seq, d = x.shape
    tile = TOKEN_TILE
    per = tile // HALO
    n_halo_blocks = seq // HALO
    n_in = w_in.shape[1]
    return pl.pallas_call(
        _layer0_body,
        out_shape=jax.ShapeDtypeStruct(x.shape, _F32),
        grid=(bsz, seq // tile),
        in_specs=[
            pl.BlockSpec((None, HALO, d), lambda b, t: (b, jnp.maximum(t * per - 1, 0), 0)),
            pl.BlockSpec((None, tile, d), lambda b, t: (b, t, 0)),
            pl.BlockSpec((None, HALO, d), lambda b, t: (b, jnp.minimum((t + 1) * per, n_halo_blocks - 1), 0)),
            pl.BlockSpec((None, 1, N_MOD * d), lambda b, t: (b + b_off, 0, 0)),
            _const_spec((1, d)),
            _const_spec((d, n_in)),
            _const_spec((K_A, SUBLANES, D_A)),
            _const_spec((K_B, SUBLANES, D_B)),
            _const_spec((1, D_B)),
            _const_spec((1, D_B)),
            _const_spec((1, D_B)),
            _const_spec((D_A + D_B, d)),
            _const_spec((1, d)),
            _layer_spec((d, 2 * D_FF), 0),
            _layer_spec((D_FF, d), 0),
            _const_spec((SUBLANES, LANES)),
            pl.BlockSpec(memory_space=pltpu.SMEM),
        ],
        out_specs=pl.BlockSpec((None, tile, d), lambda b, t: (b, t, 0)),
        scratch_shapes=[pltpu.VMEM((1, tile + 2 * HALO, n_in), _F32),
                        pltpu.VMEM((D_A // LANES, tile + 2 * HALO, LANES), _F32),
                        pltpu.VMEM((D_B // LANES, tile + 2 * HALO, LANES), _F32),
                        pltpu.VMEM((D_A // LANES, tile, LANES), _F32),
                        pltpu.VMEM((D_B // LANES, tile, LANES), _F32)],
        compiler_params=_params(2),
        name="layer0",
    )(x, x, x, mod, ng, w_in, conv_a, conv_b_w, conv_b_b, ln_g, ln_b, w_out, fg, f_w_in, f_w_out,
      jnp.zeros((SUBLANES, LANES), jnp.int32), jnp.zeros((1,), jnp.int32))


def _ffn_final_body(x_ref, yd_ref, mod_ref, ng_ref, win_ref, wout_ref, wd_ref, fg_ref, o_ref):
    mod = mod_ref[...]
    g1 = mod[:, 2 * D_MODEL:3 * D_MODEL]
    yd = jnp.concatenate([yd_ref[0], yd_ref[1]], axis=1).astype(_BF16)
    x = x_ref[...] + g1 * _dot(yd, wd_ref[...])
    x = _swiglu_residual(x, mod, ng_ref, win_ref, wout_ref)
    ms = jnp.mean(x * x, axis=-1, keepdims=True)
    o_ref[...] = x * lax.rsqrt(ms + EPS) * fg_ref[...]


def _ffn_final_call(x, yd, mod, b_off, ng, w_in, w_out, w_d, final_g):
    bsz, seq, d = x.shape
    tile = WIDE_TOKEN_TILE
    return pl.pallas_call(
        _ffn_final_body,
        out_shape=jax.ShapeDtypeStruct(x.shape, _F32),
        grid=(bsz, seq // tile),
        in_specs=[
            pl.BlockSpec((None, tile, d), lambda b, t: (b, t, 0)),
            pl.BlockSpec((None, 2, tile, LANES), lambda b, t: (b, 0, t, 0)),
            pl.BlockSpec((None, 1, N_MOD * d), lambda b, t: (b + b_off, 0, 0)),
            _const_spec((1, d)),
            _layer_spec((d, 2 * D_FF), 1),
            _layer_spec((D_FF, d), 1),
            _const_spec((D_D, d)),
            _const_spec((1, d)),
        ],
        out_specs=pl.BlockSpec((None, tile, d), lambda b, t: (b, t, 0)),
        compiler_params=_params(2),
        name="ffn_final",
    )(x, yd, mod, ng, w_in, w_out, w_d, final_g)


def _mix_cd_body(x_ref, mod_ref, ng_ref, win_ref, lng_ref, lnb_ref, ws_ref, bs_ref, woutc_ref,
                 cdft_ref, o_ref, z_ref):
    tile = x_ref.shape[0]
    n_chunks = tile // CHUNK
    x = x_ref[...]
    sh1, sc1, g1, _, _, _ = _mod_slices(mod_ref[...])
    h = _modulated_rmsnorm(x, ng_ref[...], sc1, sh1).astype(_BF16)
    p = _dot(h, win_ref[...])
    u = p[:, 0:D_C]
    v = _layernorm(p[:, D_C:2 * D_C], lng_ref[...], lnb_ref[...]).astype(_BF16)
    f = p[:, 2 * D_C:].astype(_BF16)

    head_cols = []
    for hd in range(H_C):
        lo = hd * CHUNK
        rhs = jnp.concatenate([v[n * CHUNK:(n + 1) * CHUNK, lo:lo + CHUNK] for n in range(n_chunks)], axis=1)
        sv = _dot(ws_ref[hd], rhs)
        bias = bs_ref[hd]
        head_cols.append(jnp.concatenate(
            [sv[:, n * CHUNK:(n + 1) * CHUNK] + bias for n in range(n_chunks)], axis=0))
    y_c = (u * jnp.concatenate(head_cols, axis=1)).astype(_BF16)
    o_ref[...] = x + g1 * _dot(y_c, woutc_ref[...])

    z = _dot(f, cdft_ref[...])
    for part in range(2):
        for half in range(2):
            lo = (2 * part + half) * LANES
            z_ref[part, half] = z[:, lo:lo + LANES]


def _mix_cd_call(x, mod, b_off, ng, w_in, ln_g, ln_b, w_s, b_s_rows, w_out_c, cdft):
    bsz, seq, d = x.shape
    tile = WIDE_TOKEN_TILE
    n_in = w_in.shape[1]
    return pl.pallas_call(
        _mix_cd_body,
        out_shape=(jax.ShapeDtypeStruct(x.shape, _F32),
                   jax.ShapeDtypeStruct((bsz, 2, 2, seq, LANES), _F32)),
        grid=(bsz, seq // tile),
        in_specs=[
            pl.BlockSpec((None, tile, d), lambda b, t: (b, t, 0)),
            pl.BlockSpec((None, 1, N_MOD * d), lambda b, t: (b + b_off, 0, 0)),
            _const_spec((1, d)),
            _const_spec((d, n_in)),
            _const_spec((1, D_C)),
            _const_spec((1, D_C)),
            _const_spec((H_C, CHUNK, CHUNK)),
            _const_spec((H_C, CHUNK, CHUNK)),
            _const_spec((D_C, d)),
            _const_spec((D_D, 2 * D_D)),
        ],
        out_specs=(pl.BlockSpec((None, tile, d), lambda b, t: (b, t, 0)),
                   pl.BlockSpec((None, 2, 2, tile, LANES), lambda b, t: (b, 0, 0, t, 0))),
        compiler_params=_params(2),
        name="mixer_cd",
    )(x, mod, ng, w_in, ln_g, ln_b, w_s, b_s_rows, w_out_c, cdft)


def _seq_dft_body(z_ref, w1_ref, m_ref, y_ref, a_scr, g_scr, s_scr):
    step = pl.program_id(2)
    n1 = z_ref.shape[1]
    n2 = a_scr.shape[2]
    per_step = z_ref.shape[2]
    n_stage1 = n2 // per_step

    @pl.when(step < n_stage1)
    def _stage1():
        for grp in range(per_step // SUBLANES):
            rows = slice(grp * SUBLANES, (grp + 1) * SUBLANES)
            for part in range(2):
                g_scr[grp, part] = z_ref[part, :, rows, :].reshape(n1 * SUBLANES, LANES)
            cols = [jnp.concatenate([g_scr[grp, part, pl.ds(i, n1, stride=SUBLANES), :] for part in range(2)], axis=0)
                    for i in range(SUBLANES)]
            rhs = jnp.concatenate(cols, axis=1).astype(_BF16)
            a = _dot(w1_ref[...], rhs)
            for part in range(2):
                for i in range(SUBLANES):
                    g_scr[grp, part, pl.ds(i, n1, stride=SUBLANES), :] = (
                        a[part * n1:(part + 1) * n1, i * LANES:(i + 1) * LANES])
            col0 = pl.multiple_of(step * per_step + grp * SUBLANES, SUBLANES)
            for part in range(2):
                a_scr[part, :, pl.ds(col0, SUBLANES), :] = g_scr[grp, part].reshape(n1, SUBLANES, LANES)

    @pl.when(step >= n_stage1)
    def _stage2():
        k0 = (step - n_stage1) * per_step
        for grp in range(per_step // SUBLANES):
            for i in range(SUBLANES):
                k1 = grp * SUBLANES + i
                rhs = jnp.concatenate([a_scr[0, k0 + k1], a_scr[1, k0 + k1]], axis=0).astype(_BF16)
                s_scr[grp, pl.ds(i, n2, stride=SUBLANES), :] = _dot(m_ref[k1], rhs)
            y_ref[:, grp * SUBLANES:(grp + 1) * SUBLANES, :] = s_scr[grp].reshape(n2, SUBLANES, LANES)


def _seq_dft_call(z, w1, m2):
    bsz, _, _, seq, _ = z.shape
    n2 = DFT_N2
    n1 = seq // n2
    per_step = DFT_BLOCK_ROWS // n1
    n_stage1 = n2 // per_step
    n_stage2 = n1 // per_step
    groups = per_step // SUBLANES
    z6 = z.reshape(bsz, 2, 2, n1, n2, LANES)
    y = pl.pallas_call(
        _seq_dft_body,
        out_shape=jax.ShapeDtypeStruct((bsz, 2, n2, n1, LANES), _F32),
        grid=(bsz, 2, n_stage1 + n_stage2),
        in_specs=[
            pl.BlockSpec((None, 2, None, n1, per_step, LANES),
                         lambda b, hf, st: (b, 0, hf, 0, jnp.minimum(st, n_stage1 - 1), 0)),
            _const_spec((2 * n1, 2 * n1)),
            pl.BlockSpec((per_step, n2, 2 * n2), lambda b, hf, st: (jnp.maximum(st - n_stage1, 0), 0, 0)),
        ],
        out_specs=pl.BlockSpec((None, None, n2, per_step, LANES),
                               lambda b, hf, st: (b, hf, 0, jnp.maximum(st - n_stage1, 0), 0)),
        scratch_shapes=[pltpu.VMEM((2, n1, n2, LANES), _F32),
                        pltpu.VMEM((groups, 2, n1 * SUBLANES, LANES), _F32),
                        pltpu.VMEM((groups, n2 * SUBLANES, LANES), _F32)],
        compiler_params=pltpu.CompilerParams(
            dimension_semantics=("parallel", "parallel", "arbitrary"), vmem_limit_bytes=VMEM_LIMIT),
        name="seq_dft",
    )(z6, w1, m2)
    return y.reshape(bsz, 2, seq, LANES)


def _channel_dft_matrix():
    c = np.arange(D_D)
    same = (c[:, None] // DG_D) == (c[None, :] // DG_D)
    ang = 2.0 * np.pi * ((c[:, None] % DG_D) * (c[None, :] % DG_D) % DG_D) / DG_D
    scale = 1.0 / np.sqrt(DG_D)
    re = np.where(same, np.cos(ang), 0.0) * scale
    im = np.where(same, -np.sin(ang), 0.0) * scale
    return np.concatenate([re, im], axis=1)


def _stage1_matrix(n1):
    k = np.arange(n1)
    ang = 2.0 * np.pi * ((k[:, None] * k[None, :]) % n1) / n1
    c, s = np.cos(ang), np.sin(ang)
    return np.block([[c, s], [-s, c]])


def _stage2_matrices(n1, n2):
    seq = n1 * n2
    k1 = np.arange(n1)[:, None, None]
    k2 = np.arange(n2)[None, :, None]
    nn = np.arange(n2)[None, None, :]
    ang = 2.0 * np.pi * ((nn * (k1 + n1 * k2)) % seq) / seq
    scale = 1.0 / np.sqrt(seq)
    return np.concatenate([np.cos(ang), np.sin(ang)], axis=2) * scale


def _trunk(x, b_off, mod, wts):
    seq = x.shape[1]
    n1 = seq // DFT_N2
    x = _layer0_call(x, mod[0], b_off, wts["mix_g"][0], wts["ab_w_in"], wts["ab_conv_a"], wts["ab_conv_b_w"],
                     wts["ab_conv_b_b"], wts["ab_ln_g"], wts["ab_ln_b"], wts["ab_w_out"],
                     wts["ffn_g"][0], wts["ffn_w_in"], wts["ffn_w_out"])
    x, z = _mix_cd_call(x, mod[1], b_off, wts["mix_g"][1], wts["cd_w_in"], wts["cd_ln_g"], wts["cd_ln_b"],
                        wts["cd_w_s"], wts["cd_b_s_rows"], wts["cd_w_out_c"], wts["cdft"])
    yd = _seq_dft_call(z, _table(_stage1_matrix(n1)), _table(_stage2_matrices(n1, DFT_N2)))
    return _ffn_final_call(x, yd, mod[1], b_off, wts["ffn_g"][1], wts["ffn_w_in"], wts["ffn_w_out"],
                           wts["cd_w_out_d"], wts["final_g"])


def _table(values):
    return jnp.asarray(values, _F32).astype(_BF16)


def _tap_rows(w):
    return jnp.broadcast_to(w[:, None, :], (w.shape[0], SUBLANES, w.shape[1]))


def kernel(x_prompt, x_sample, c_prompt, c_sample, ada_w, ada_b, mix_norm_g, ffn_norm_g, ab_w_in, ab_conv_a,
           ab_conv_b_w, ab_conv_b_b, ab_ln_g, ab_ln_b, ab_w_out, cd_w_in, cd_ln_g, cd_ln_b, cd_w_s, cd_b_s,
           cd_w_out, ffn_w_in, ffn_w_out, final_g):
    depth = ada_w.shape[0]
    assert depth == 2 and ab_w_in.shape[0] == 1 and cd_w_in.shape[0] == 1
    n_prompt, n_sample = c_prompt.shape[0], c_sample.shape[0]
    assert n_prompt + n_sample <= MOD_ROWS

    c_all = jnp.concatenate(
        [c_prompt, c_sample, jnp.zeros((MOD_ROWS - n_prompt - n_sample, D_MODEL), _F32)], axis=0)
    mod = _mod_call(c_all, ada_w, ada_b).reshape(depth, MOD_ROWS, 1, N_MOD * D_MODEL)

    wts = {
        "mix_g": mix_norm_g.reshape(depth, 1, D_MODEL),
        "ffn_g": ffn_norm_g.reshape(depth, 1, D_MODEL),
        "ab_w_in": ab_w_in[0].astype(_BF16),
        "ab_conv_a": _tap_rows(ab_conv_a[0]),
        "ab_conv_b_w": _tap_rows(ab_conv_b_w[0]),
        "ab_conv_b_b": ab_conv_b_b,
        "ab_ln_g": ab_ln_g,
        "ab_ln_b": ab_ln_b,
        "ab_w_out": ab_w_out[0].astype(_BF16),
        "cd_w_in": cd_w_in[0].astype(_BF16),
        "cd_ln_g": cd_ln_g,
        "cd_ln_b": cd_ln_b,
        "cd_w_s": cd_w_s[0].astype(_BF16),
        "cd_b_s_rows": jnp.broadcast_to(cd_b_s[0][:, :, None], (H_C, CHUNK, CHUNK)),
        "cd_w_out_c": cd_w_out[0, :D_C].astype(_BF16),
        "cd_w_out_d": cd_w_out[0, D_C:].astype(_BF16),
        "cdft": _table(_channel_dft_matrix()),
        "ffn_w_in": ffn_w_in.astype(_BF16),
        "ffn_w_out": ffn_w_out.astype(_BF16),
        "final_g": final_g.reshape(1, D_MODEL),
    }
    y_prompt = _trunk(x_prompt, 0, mod, wts)
    y_sample = _trunk(x_sample, n_prompt, mod, wts)
    return (y_prompt, y_sample)
```
